```python
import jax, jax.numpy as jnp
from jax import lax
import numpy as np

D_MODEL = 2048
BATCH = 4
SEQ = 2048
DEPTH = 1

CHUNK = 64
Q_BLOCK = 128
SB_HEADS = 8
SB_HEAD_DIM = 128
SB_WIDTH = SB_HEADS * SB_HEAD_DIM
MLA_HEADS = 8
MLA_NOPE_DIM = 128
MLA_ROPE_DIM = 64
MLA_V_DIM = 128
MLA_Q_RANK = 512
MLA_KV_RANK = 256
MLA_QK_DIM = MLA_NOPE_DIM + MLA_ROPE_DIM
MLA_WIDTH = MLA_HEADS * MLA_V_DIM
MIX_WIDTH = SB_WIDTH + MLA_WIDTH
IN_COLS = 3 * SB_WIDTH + MLA_Q_RANK + MLA_KV_RANK + MLA_ROPE_DIM
D_FF = 5632
CONV_WIDTH = 3
ROPE_THETA = 10000.0
EPS = 1e-6

kernel_name = "hymba_stickbreak_mla_convffn_block"


def rmsnorm(x, g):
    xf = x.astype(jnp.float32)
    y = xf * lax.rsqrt(jnp.mean(xf * xf, axis=-1, keepdims=True) + EPS)
    return (y * g.astype(jnp.float32)).astype(x.dtype)


def rope_tables(positions):
    half = MLA_ROPE_DIM // 2
    inv_freq = ROPE_THETA ** (-jnp.arange(half, dtype=jnp.float32) / half)
    ang = positions.astype(jnp.float32)[..., None] * inv_freq
    return jnp.cos(ang), jnp.sin(ang)


def apply_rope(x, cos, sin):
    x1, x2 = jnp.split(x.astype(jnp.float32), 2, axis=-1)
    return jnp.concatenate([x1 * cos - x2 * sin, x1 * sin + x2 * cos], axis=-1).astype(x.dtype)


def stick_breaking_attention(q, k, v):
    S, Dh = q.shape[1], q.shape[-1]
    scale = Dh ** -0.5
    outs = []
    for q0 in range(0, S, Q_BLOCK):
        q_end = q0 + Q_BLOCK
        z = jnp.einsum('bqhd,bkhd->bhqk', q[:, q0:q_end], k[:, :q_end]).astype(jnp.float32) * scale
        t = q0 + jnp.arange(Q_BLOCK)
        s = jnp.arange(q_end)
        mask = s[None, :] < t[:, None]
        log_1m = jnp.where(mask, -jax.nn.softplus(z), 0.0)
        after = lax.cumsum(log_1m, axis=3, reverse=True) - log_1m
        a = jnp.where(mask, jnp.exp(jax.nn.log_sigmoid(z) + after), 0.0)
        outs.append(jnp.einsum('bhqk,bkhd->bqhd', a.astype(v.dtype), v[:, :q_end]))
    return jnp.concatenate(outs, axis=1)


def mla_attention(q_nope, q_rope, k_nope, k_rope, v):
    S = q_nope.shape[1]
    scale = MLA_QK_DIM ** -0.5
    outs = []
    for q0 in range(0, S, Q_BLOCK):
        q_end = q0 + Q_BLOCK
        s_nope = jnp.einsum('bqhd,bkhd->bhqk', q_nope[:, q0:q_end], k_nope[:, :q_end])
        s_rope = jnp.einsum('bqhd,bkd->bhqk', q_rope[:, q0:q_end], k_rope[:, :q_end])
        scores = (s_nope.astype(jnp.float32) + s_rope.astype(jnp.float32)) * scale
        t_chunk = (q0 + jnp.arange(Q_BLOCK)) // CHUNK
        s_chunk = jnp.arange(q_end) // CHUNK
        mask = s_chunk[None, :] <= t_chunk[:, None]
        p = jax.nn.softmax(jnp.where(mask, scores, -jnp.inf), axis=-1)
        outs.append(jnp.einsum('bhqk,bkhd->bqhd', p.astype(v.dtype), v[:, :q_end]))
    return jnp.concatenate(outs, axis=1)


def causal_depthwise_conv(u, w, b):
    S = u.shape[1]
    up = jnp.pad(u, ((0, 0), (CONV_WIDTH - 1, 0), (0, 0)))
    y = up[:, 0:S] * w[0]
    for j in range(1, CONV_WIDTH):
        y = y + up[:, j:j + S] * w[j]
    return y + b


def setup_inputs(seed: int = 0) -> dict:
    key = jax.random.key(seed)
    ks = jax.random.split(key, 20)
    f32 = jnp.float32
    L = DEPTH

    def wt(k, shape, fan_in):
        return jax.random.normal(k, shape, f32) * fan_in ** -0.5

    def gain(k, shape):
        return 1.0 + 0.02 * jax.random.normal(k, shape, f32)

    x = jax.random.normal(ks[0], (BATCH, SEQ, D_MODEL), f32)
    positions = jnp.tile(jnp.arange(SEQ, dtype=jnp.int32)[None, :], (BATCH, 1))
    return {
        "x": x,
        "positions": positions,
        "g_attn_pre": gain(ks[1], (L, D_MODEL)),
        "w_in": wt(ks[2], (L, D_MODEL, IN_COLS), D_MODEL),
        "g_cq": gain(ks[3], (L, MLA_Q_RANK)),
        "w_uq": wt(ks[4], (L, MLA_Q_RANK, MLA_HEADS * MLA_QK_DIM), MLA_Q_RANK),
        "g_ckv": gain(ks[5], (L, MLA_KV_RANK)),
        "w_ukv": wt(ks[6], (L, MLA_KV_RANK, MLA_HEADS * (MLA_NOPE_DIM + MLA_V_DIM)), MLA_KV_RANK),
        "g_out_sb": gain(ks[7], (L, SB_WIDTH)),
        "g_out_mla": gain(ks[8], (L, MLA_WIDTH)),
        "w_o": wt(ks[9], (L, MIX_WIDTH, D_MODEL), MIX_WIDTH),
        "g_attn_post": gain(ks[10], (L, D_MODEL)),
        "g_ffn_pre": gain(ks[11], (L, D_MODEL)),
        "w_up": wt(ks[12], (L, D_MODEL, 2 * D_FF), D_MODEL),
        "conv_w": wt(ks[13], (L, CONV_WIDTH, 2 * D_FF), CONV_WIDTH),
        "conv_b": 0.02 * jax.random.normal(ks[14], (L, 2 * D_FF), f32),
        "w_down": wt(ks[15], (L, D_FF, D_MODEL), D_FF),
        "g_ffn_post": gain(ks[16], (L, D_MODEL)),
    }


def reference(x, positions, g_attn_pre, w_in, g_cq, w_uq, g_ckv, w_ukv, g_out_sb, g_out_mla,
              w_o, g_attn_post, g_ffn_pre, w_up, conv_w, conv_b, w_down, g_ffn_post):
    B, S, _ = x.shape
    cos, sin = rope_tables(positions)
    split_points = np.cumsum([SB_WIDTH, SB_WIDTH, SB_WIDTH, MLA_Q_RANK, MLA_KV_RANK])
    split_points = split_points.tolist()
    for l in range(DEPTH):
        h = rmsnorm(x, g_attn_pre[l])
        proj = h @ w_in[l]
        q_sb, k_sb, v_sb, c_q, c_kv, k_rope = jnp.split(proj, split_points, axis=-1)

        o_sb = stick_breaking_attention(
            q_sb.reshape(B, S, SB_HEADS, SB_HEAD_DIM),
            k_sb.reshape(B, S, SB_HEADS, SB_HEAD_DIM),
            v_sb.reshape(B, S, SB_HEADS, SB_HEAD_DIM)).reshape(B, S, SB_WIDTH)

        q = (rmsnorm(c_q, g_cq[l]) @ w_uq[l]).reshape(B, S, MLA_HEADS, MLA_QK_DIM)
        q_nope, q_rope = jnp.split(q, [MLA_NOPE_DIM], axis=-1)
        q_rope = apply_rope(q_rope, cos[:, :, None, :], sin[:, :, None, :])
        kv = (rmsnorm(c_kv, g_ckv[l]) @ w_ukv[l]).reshape(B, S, MLA_HEADS, MLA_NOPE_DIM + MLA_V_DIM)
        k_nope, v_mla = jnp.split(kv, [MLA_NOPE_DIM], axis=-1)
        k_rope = apply_rope(k_rope, cos, sin)
        o_mla = mla_attention(q_nope, q_rope, k_nope, k_rope, v_mla).reshape(B, S, MLA_WIDTH)

        mixed = jnp.concatenate([rmsnorm(o_sb, g_out_sb[l]), rmsnorm(o_mla, g_out_mla[l])], axis=-1)
        x = x + rmsnorm(mixed @ w_o[l], g_attn_post[l])

        h = rmsnorm(x, g_ffn_pre[l])
        u = causal_depthwise_conv(h @ w_up[l], conv_w[l], conv_b[l])
        gate, val = jnp.split(u, 2, axis=-1)
        y = (jax.nn.gelu(gate, approximate=True) * val) @ w_down[l]
        x = x + rmsnorm(y, g_ffn_post[l])
    return x
```

```python
import functools

import jax
import jax.numpy as jnp
from jax import lax
from jax.experimental import pallas as pl
from jax.experimental.pallas import tpu as pltpu

F32 = jnp.float32
BF16 = jnp.bfloat16

D_MODEL = 2048
CHUNK = 64
SB_HEADS = 8
SB_HEAD_DIM = 128
SB_WIDTH = SB_HEADS * SB_HEAD_DIM
MLA_HEADS = 8
MLA_NOPE_DIM = 128
MLA_ROPE_DIM = 64
MLA_V_DIM = 128
MLA_Q_RANK = 512
MLA_KV_RANK = 256
MLA_QK_DIM = MLA_NOPE_DIM + MLA_ROPE_DIM
MLA_WIDTH = MLA_HEADS * MLA_V_DIM
D_FF = 5632
CONV_WIDTH = 3
ROPE_THETA = 10000.0
EPS = 1e-6

LANES = 128
BF16_SUBLANES = 16
MLA_QK_PAD = 2 * LANES

_C_QSB = 0
_C_KSB = SB_WIDTH
_C_VSB = 2 * SB_WIDTH
_C_CQ = 3 * SB_WIDTH
_C_CKV = _C_CQ + MLA_Q_RANK
_C_KR = _C_CKV + MLA_KV_RANK
IN_COLS_PAD = _C_KR + LANES

PROJ_ROWS = 256
ATTN_TILE = 256
OUT_ROWS = 256
FFN_ROWS = 512
FFN_COLS = 512
HALO = BF16_SUBLANES

VMEM_LIMIT = 56 * 1024 * 1024


def _rmsnorm(x, g):
    return x * lax.rsqrt(jnp.mean(x * x, axis=-1, keepdims=True) + EPS) * g


def _dot(a, b):
    return jnp.dot(a, b, preferred_element_type=F32)


def _dot_nt(a, b):
    return lax.dot_general(a, b, (((1,), (1,)), ((), ())), preferred_element_type=F32)


def _proj_kernel(x_ref, pos_ref, invf_ref, g_pre_ref, w_in_ref, g_cq_ref, w_uq_ref, g_ckv_ref, w_ukv_ref,
                 qsb_ref, ksb_ref, vsb_ref, qm_ref, kn_ref, kr_ref, vm_ref):
    sb_scale = SB_HEAD_DIM ** -0.5
    mla_scale = MLA_QK_DIM ** -0.5
    h = _rmsnorm(x_ref[...], g_pre_ref[...]).astype(BF16)

    def proj(lo, hi):
        return _dot(h, w_in_ref[:, lo:hi])

    qsb_ref[...] = (proj(_C_QSB, _C_KSB) * sb_scale).astype(BF16)
    ksb_ref[...] = proj(_C_KSB, _C_VSB).astype(BF16)
    vsb_ref[...] = proj(_C_VSB, _C_CQ).astype(BF16)

    ang = pos_ref[...].astype(F32) * invf_ref[...]
    lane = lax.broadcasted_iota(jnp.int32, ang.shape, 1)
    cos = jnp.cos(ang)
    sin = jnp.sin(ang)
    t1 = jnp.where(lane < MLA_ROPE_DIM, cos, 0.0)
    t2 = jnp.where(lane < MLA_ROPE_DIM // 2, -sin, jnp.where(lane < MLA_ROPE_DIM, sin, 0.0))

    def rope(g):
        return g * t1 + pltpu.roll(g, MLA_ROPE_DIM, axis=1) * t2

    kr_ref[...] = rope(proj(_C_KR, IN_COLS_PAD)).astype(BF16)

    cq = _rmsnorm(proj(_C_CQ, _C_CKV), g_cq_ref[...]).astype(BF16)
    for hd in range(MLA_HEADS):
        c0 = hd * MLA_QK_PAD
        qh = _dot(cq, w_uq_ref[:, c0:c0 + MLA_QK_PAD])
        qm_ref[:, c0:c0 + LANES] = (qh[:, :LANES] * mla_scale).astype(BF16)
        qm_ref[:, c0 + LANES:c0 + MLA_QK_PAD] = (rope(qh[:, LANES:]) * mla_scale).astype(BF16)

    ckv = _rmsnorm(proj(_C_CKV, _C_KR), g_ckv_ref[...]).astype(BF16)
    kn_ref[...] = _dot(ckv, w_ukv_ref[:, :MLA_WIDTH]).astype(BF16)
    vm_ref[...] = _dot(ckv, w_ukv_ref[:, MLA_WIDTH:]).astype(BF16)


def _proj(x2, pos2, invf, g_pre, w_in_p, g_cq, w_uq_p, g_ckv, w_ukv_p):
    n = x2.shape[0]
    tm = PROJ_ROWS
    row = lambda w: pl.BlockSpec((tm, w), lambda i: (i, 0))
    full = lambda a: pl.BlockSpec(a.shape, lambda i: (0, 0))
    out_widths = (SB_WIDTH, SB_WIDTH, SB_WIDTH, MLA_HEADS * MLA_QK_PAD, MLA_WIDTH, LANES, MLA_WIDTH)
    return pl.pallas_call(
        _proj_kernel,
        grid=(n // tm,),
        in_specs=[row(D_MODEL), row(1), full(invf), full(g_pre), full(w_in_p), full(g_cq), full(w_uq_p),
                  full(g_ckv), full(w_ukv_p)],
        out_specs=[row(w) for w in out_widths],
        out_shape=[jax.ShapeDtypeStruct((n, w), BF16) for w in out_widths],
        compiler_params=pltpu.CompilerParams(dimension_semantics=("arbitrary",), vmem_limit_bytes=VMEM_LIMIT),
        name="proj",
    )(x2, pos2, invf, g_pre, w_in_p, g_cq, w_uq_p, g_ckv, w_ukv_p)


def _softplus(z):
    return jnp.maximum(z, 0.0) + jnp.log1p(jnp.exp(-jnp.abs(z)))


def _sb_kernel(q_ref, k_ref, v_ref, o_ref, acc_ref, carry_ref):
    t = ATTN_TILE
    i = pl.program_id(2)
    q = q_ref[...]
    row = lax.broadcasted_iota(jnp.int32, (t, t), 0)
    col = lax.broadcasted_iota(jnp.int32, (t, t), 1)
    tri = (row >= col).astype(BF16)
    tri2 = jnp.concatenate([tri, tri], axis=0)

    def block(kb, vb, carry, mask):
        z = _dot_nt(q, kb)
        log_1m = -_softplus(z)
        if mask is not None:
            log_1m = jnp.where(mask, log_1m, 0.0)
        hi = log_1m.astype(BF16)
        lo = (log_1m - hi.astype(F32)).astype(BF16)
        cum = _dot(jnp.concatenate([hi, lo], axis=1), tri2) + carry
        a = jnp.exp(z + cum)
        if mask is not None:
            a = jnp.where(mask, a, 0.0)
        return _dot(a.astype(BF16), vb), cum[:, 0:1]

    start = pl.multiple_of(i * t, t)
    pv, carry = block(k_ref[pl.ds(start, t), :], v_ref[pl.ds(start, t), :], 0.0, col < row)
    acc_ref[...] = pv
    carry_ref[...] = carry

    def body(it, _):
        s0 = pl.multiple_of((i - 1 - it) * t, t)
        pv, carry = block(k_ref[pl.ds(s0, t), :], v_ref[pl.ds(s0, t), :], carry_ref[...], None)
        acc_ref[...] += pv
        carry_ref[...] = carry
        return 0

    lax.fori_loop(0, i, body, 0)
    o_ref[...] = acc_ref[...]


def _sb_attention(q, k, v):
    b, s, _ = q.shape
    t = ATTN_TILE
    d = SB_HEAD_DIM
    return pl.pallas_call(
        _sb_kernel,
        grid=(b, SB_HEADS, s // t),
        in_specs=[pl.BlockSpec((None, t, d), lambda bi, h, i: (bi, i, h)),
                  pl.BlockSpec((None, s, d), lambda bi, h, i: (bi, 0, h)),
                  pl.BlockSpec((None, s, d), lambda bi, h, i: (bi, 0, h))],
        out_specs=pl.BlockSpec((None, t, d), lambda bi, h, i: (bi, i, h)),
        out_shape=jax.ShapeDtypeStruct((b, s, SB_WIDTH), F32),
        scratch_shapes=[pltpu.VMEM((t, d), F32), pltpu.VMEM((t, 1), F32)],
        compiler_params=pltpu.CompilerParams(dimension_semantics=("arbitrary",) * 3, vmem_limit_bytes=VMEM_LIMIT),
        name="sb_attn",
    )(q, k, v)


def _mla_kernel(q_ref, kn_ref, kr_ref, v_ref, o_ref, acc_ref, m_ref):
    t = ATTN_TILE
    i = pl.program_id(2)
    q = q_ref[...]
    ones = jnp.ones((t, LANES), BF16)

    def block(s0, m_prev, mask):
        kcat = jnp.concatenate([kn_ref[pl.ds(s0, t), :], kr_ref[pl.ds(s0, t), :]], axis=1)
        sc = _dot_nt(q, kcat)
        if mask is not None:
            sc = jnp.where(mask, sc, -jnp.inf)
        m_new = jnp.maximum(m_prev, jnp.max(sc, axis=-1, keepdims=True))
        p = jnp.exp(sc - m_new)
        vcat = jnp.concatenate([v_ref[pl.ds(s0, t), :], ones], axis=1)
        return _dot(p.astype(BF16), vcat), m_new

    row = lax.broadcasted_iota(jnp.int32, (t, t), 0)
    col = lax.broadcasted_iota(jnp.int32, (t, t), 1)
    pv, m = block(pl.multiple_of(i * t, t), -jnp.inf, (col // CHUNK) <= (row // CHUNK))
    acc_ref[...] = pv
    m_ref[...] = m

    def body(it, _):
        m_prev = m_ref[...]
        pv, m_new = block(pl.multiple_of((i - 1 - it) * t, t), m_prev, None)
        acc_ref[...] = acc_ref[...] * jnp.exp(m_prev - m_new) + pv
        m_ref[...] = m_new
        return 0

    lax.fori_loop(0, i, body, 0)
    acc = acc_ref[...]
    o_ref[...] = acc[:, :MLA_V_DIM] / acc[:, MLA_V_DIM:]


def _mla_attention(q, kn, kr, v):
    b, s, _ = kn.shape
    t = ATTN_TILE
    return pl.pallas_call(
        _mla_kernel,
        grid=(b, MLA_HEADS, s // t),
        in_specs=[pl.BlockSpec((None, t, MLA_QK_PAD), lambda bi, h, i: (bi, i, h)),
                  pl.BlockSpec((None, s, MLA_NOPE_DIM), lambda bi, h, i: (bi, 0, h)),
                  pl.BlockSpec((None, s, LANES), lambda bi, h, i: (bi, 0, 0)),
                  pl.BlockSpec((None, s, MLA_V_DIM), lambda bi, h, i: (bi, 0, h))],
        out_specs=pl.BlockSpec((None, t, MLA_V_DIM), lambda bi, h, i: (bi, i, h)),
        out_shape=jax.ShapeDtypeStruct((b, s, MLA_WIDTH), F32),
        scratch_shapes=[pltpu.VMEM((t, MLA_V_DIM + LANES), F32), pltpu.VMEM((t, 1), F32)],
        compiler_params=pltpu.CompilerParams(dimension_semantics=("arbitrary",) * 3, vmem_limit_bytes=VMEM_LIMIT),
        name="mla_attn",
    )(q, kn, kr, v)


def _out_kernel(osb_ref, omla_ref, x_ref, g_sb_ref, g_mla_ref, w_o_ref, g_post_ref, g_ffn_ref, x1_ref, h2_ref):
    n_sb = _rmsnorm(osb_ref[...], g_sb_ref[...]).astype(BF16)
    n_mla = _rmsnorm(omla_ref[...], g_mla_ref[...]).astype(BF16)
    y = _dot(n_sb, w_o_ref[:SB_WIDTH, :]) + _dot(n_mla, w_o_ref[SB_WIDTH:, :])
    x1 = x_ref[...] + _rmsnorm(y, g_post_ref[...])
    x1_ref[...] = x1
    h2_ref[...] = _rmsnorm(x1, g_ffn_ref[...]).astype(BF16)


def _out_proj(o_sb, o_mla, x2, g_sb, g_mla, w_o, g_post, g_ffn):
    n = x2.shape[0]
    tm = OUT_ROWS
    row = lambda w: pl.BlockSpec((tm, w), lambda i: (i, 0))
    full = lambda a: pl.BlockSpec(a.shape, lambda i: (0, 0))
    return pl.pallas_call(
        _out_kernel,
        grid=(n // tm,),
        in_specs=[row(SB_WIDTH), row(MLA_WIDTH), row(D_MODEL), full(g_sb), full(g_mla), full(w_o), full(g_post),
                  full(g_ffn)],
        out_specs=[row(D_MODEL), row(D_MODEL)],
        out_shape=[jax.ShapeDtypeStruct((n, D_MODEL), F32), jax.ShapeDtypeStruct((n, D_MODEL), BF16)],
        compiler_params=pltpu.CompilerParams(dimension_semantics=("arbitrary",), vmem_limit_bytes=VMEM_LIMIT),
        name="out_proj",
    )(o_sb, o_mla, x2, g_sb, g_mla, w_o, g_post, g_ffn)


def _ffn_kernel(seq_tiles, h_ref, halo_ref, x1_ref, wg_ref, wv_ref, cwg_ref, cwv_ref, cbg_ref, cbv_ref, wd_ref,
                g_post_ref, o_ref, hcat_ref, p_ref, acc_ref):
    tm = FFN_ROWS
    i = pl.program_id(0)
    j = pl.program_id(1)

    @pl.when(j == 0)
    def _():
        first = (i % seq_tiles) == 0
        hcat_ref[0:HALO, :] = jnp.where(first, jnp.zeros_like(halo_ref[...]), halo_ref[...])
        hcat_ref[HALO:, :] = h_ref[...]

    def conv(w_ref, cw_ref, cb_ref):
        p_ref[...] = _dot(hcat_ref[...], w_ref[...])
        u = cb_ref[...] + cw_ref[CONV_WIDTH - 1:CONV_WIDTH, :] * p_ref[HALO:HALO + tm, :]
        for tap in range(1, CONV_WIDTH):
            k = CONV_WIDTH - 1 - tap
            u = u + cw_ref[k:k + 1, :] * p_ref[HALO - tap:HALO - tap + tm, :]
        return u

    gate = conv(wg_ref, cwg_ref, cbg_ref)
    val = conv(wv_ref, cwv_ref, cbv_ref)
    act = (jax.nn.gelu(gate, approximate=True) * val).astype(BF16)
    contrib = _dot(act, wd_ref[...])

    @pl.when(j == 0)
    def _():
        acc_ref[...] = contrib

    @pl.when(j > 0)
    def _():
        acc_ref[...] += contrib

    @pl.when(j == pl.num_programs(1) - 1)
    def _():
        o_ref[...] = x1_ref[...] + _rmsnorm(acc_ref[...], g_post_ref[...])


def _ffn(h2, x1, w_up, conv_w, conv_b, w_down, g_post, seq_len):
    n = h2.shape[0]
    tm, tn = FFN_ROWS, FFN_COLS
    nj = D_FF // tn
    halo_blocks = tm // HALO
    return pl.pallas_call(
        functools.partial(_ffn_kernel, seq_len // tm),
        grid=(n // tm, nj),
        in_specs=[pl.BlockSpec((tm, D_MODEL), lambda i, j: (i, 0)),
                  pl.BlockSpec((HALO, D_MODEL), lambda i, j: (jnp.maximum(i * halo_blocks - 1, 0), 0)),
                  pl.BlockSpec((tm, D_MODEL), lambda i, j: (i, 0)),
                  pl.BlockSpec((D_MODEL, tn), lambda i, j: (0, j)),
                  pl.BlockSpec((D_MODEL, tn), lambda i, j: (0, j + nj)),
                  pl.BlockSpec((CONV_WIDTH, tn), lambda i, j: (0, j)),
                  pl.BlockSpec((CONV_WIDTH, tn), lambda i, j: (0, j + nj)),
                  pl.BlockSpec((1, tn), lambda i, j: (0, j)),
                  pl.BlockSpec((1, tn), lambda i, j: (0, j + nj)),
                  pl.BlockSpec((tn, D_MODEL), lambda i, j: (j, 0)),
                  pl.BlockSpec((1, D_MODEL), lambda i, j: (0, 0))],
        out_specs=pl.BlockSpec((tm, D_MODEL), lambda i, j: (i, 0)),
        out_shape=jax.ShapeDtypeStruct((n, D_MODEL), F32),
        scratch_shapes=[pltpu.VMEM((tm + HALO, D_MODEL), BF16), pltpu.VMEM((tm + HALO, tn), F32),
                        pltpu.VMEM((tm, D_MODEL), F32)],
        compiler_params=pltpu.CompilerParams(dimension_semantics=("arbitrary", "arbitrary"),
                                             vmem_limit_bytes=VMEM_LIMIT),
        name="ffn",
    )(h2, h2, x1, w_up, w_up, conv_w, conv_w, conv_b, conv_b, w_down, g_post)


def _swap_halves(w):
    half = w.shape[-1] // 2
    return jnp.concatenate([w[..., half:], w[..., :half]], axis=-1)


def _layout_w_in(w_in):
    k_rope = w_in[:, _C_KR:_C_KR + MLA_ROPE_DIM]
    return jnp.concatenate([w_in, _swap_halves(k_rope)], axis=1).astype(BF16)


def _layout_w_uq(w_uq):
    w = w_uq.reshape(MLA_Q_RANK, MLA_HEADS, MLA_QK_DIM)
    rope = w[:, :, MLA_NOPE_DIM:]
    w = jnp.concatenate([w[:, :, :MLA_NOPE_DIM], rope, _swap_halves(rope)], axis=-1)
    return w.reshape(MLA_Q_RANK, MLA_HEADS * MLA_QK_PAD).astype(BF16)


def _layout_w_ukv(w_ukv):
    w = w_ukv.reshape(MLA_KV_RANK, MLA_HEADS, MLA_NOPE_DIM + MLA_V_DIM)
    k_nope = w[:, :, :MLA_NOPE_DIM].reshape(MLA_KV_RANK, MLA_WIDTH)
    v = w[:, :, MLA_NOPE_DIM:].reshape(MLA_KV_RANK, MLA_WIDTH)
    return jnp.concatenate([k_nope, v], axis=1).astype(BF16)


def kernel(x, positions, g_attn_pre, w_in, g_cq, w_uq, g_ckv, w_ukv, g_out_sb, g_out_mla, w_o, g_attn_post,
           g_ffn_pre, w_up, conv_w, conv_b, w_down, g_ffn_post):
    b, s, d = x.shape
    depth = w_in.shape[0]
    n = b * s
    half = MLA_ROPE_DIM // 2
    inv_freq = ROPE_THETA ** (-jnp.arange(half, dtype=F32) / half)
    invf = jnp.tile(inv_freq, LANES // half)[None, :]
    pos2 = positions.reshape(n, 1)
    x2 = x.reshape(n, d)
    r2 = lambda g: g[None, :]
    for l in range(depth):
        qsb, ksb, vsb, qm, kn, kr, vm = _proj(
            x2, pos2, invf, r2(g_attn_pre[l]), _layout_w_in(w_in[l]), r2(g_cq[l]), _layout_w_uq(w_uq[l]),
            r2(g_ckv[l]), _layout_w_ukv(w_ukv[l]))
        b3 = lambda a: a.reshape(b, s, a.shape[-1])
        o_sb = _sb_attention(b3(qsb), b3(ksb), b3(vsb))
        o_mla = _mla_attention(b3(qm), b3(kn), b3(kr), b3(vm))
        x1, h2 = _out_proj(o_sb.reshape(n, SB_WIDTH), o_mla.reshape(n, MLA_WIDTH), x2, r2(g_out_sb[l]),
                           r2(g_out_mla[l]), w_o[l].astype(BF16), r2(g_attn_post[l]), r2(g_ffn_pre[l]))
        x2 = _ffn(h2, x1, w_up[l].astype(BF16), conv_w[l], r2(conv_b[l]), w_down[l].astype(BF16),
                  r2(g_ffn_post[l]), s)
    return x2.reshape(b, s, d)
```

```python
import functools

import jax
import jax.numpy as jnp
from jax import lax
from jax.experimental import pallas as pl
from jax.experimental.pallas import tpu as pltpu

F32 = jnp.float32
BF16 = jnp.bfloat16

D_MODEL = 2048
CHUNK = 64
SB_HEADS = 8
SB_HEAD_DIM = 128
SB_WIDTH = SB_HEADS * SB_HEAD_DIM
MLA_HEADS = 8
MLA_NOPE_DIM = 128
MLA_ROPE_DIM = 64
MLA_V_DIM = 128
MLA_Q_RANK = 512
MLA_KV_RANK = 256
MLA_QK_DIM = MLA_NOPE_DIM + MLA_ROPE_DIM
MLA_WIDTH = MLA_HEADS * MLA_V_DIM
D_FF = 5632
CONV_WIDTH = 3
ROPE_THETA = 10000.0
EPS = 1e-6
LOG2_E = 1.4426950408889634

LANES = 128
BF16_SUBLANES = 16
MLA_QK_PAD = 2 * LANES

_C_QSB = 0
_C_KSB = SB_WIDTH
_C_VSB = 2 * SB_WIDTH
_C_CQ = 3 * SB_WIDTH
_C_CKV = _C_CQ + MLA_Q_RANK
_C_KR = _C_CKV + MLA_KV_RANK

PROJ_ROWS = 256
ATTN_TILE = 256
OUT_ROWS = 256
FFN_ROWS = 512
FFN_COLS = 512
HALO = BF16_SUBLANES

VMEM_LIMIT = 56 * 1024 * 1024


def _rmsnorm(x, g):
    return x * lax.rsqrt(jnp.mean(x * x, axis=-1, keepdims=True) + EPS) * g


def _dot(a, b):
    return jnp.dot(a, b, preferred_element_type=F32)


def _dot_nt(a, b):
    return lax.dot_general(a, b, (((1,), (1,)), ((), ())), preferred_element_type=F32)


def _proj_kernel(x_ref, pos_ref, invf_ref, g_pre_ref, w_in_ref, w_kr_ref, g_cq_ref, w_uq_ref, g_ckv_ref, w_ukv_ref,
                 qsb_ref, ksb_ref, vsb_ref, qm_ref, kn_ref, kr_ref, vm_ref):
    sb_scale = SB_HEAD_DIM ** -0.5 * LOG2_E
    mla_scale = MLA_QK_DIM ** -0.5 * LOG2_E
    h = _rmsnorm(x_ref[...], g_pre_ref[...]).astype(BF16)

    def proj(lo, hi):
        return _dot(h, w_in_ref[:, lo:hi])

    qsb_ref[...] = (proj(_C_QSB, _C_KSB) * sb_scale).astype(BF16)
    ksb_ref[...] = proj(_C_KSB, _C_VSB).astype(BF16)
    vsb_ref[...] = proj(_C_VSB, _C_CQ).astype(BF16)

    ang = pos_ref[...].astype(F32) * invf_ref[...]
    lane = lax.broadcasted_iota(jnp.int32, ang.shape, 1)
    cos = jnp.cos(ang)
    sin = jnp.sin(ang)
    t1 = jnp.where(lane < MLA_ROPE_DIM, cos, 0.0)
    t2 = jnp.where(lane < MLA_ROPE_DIM // 2, -sin, jnp.where(lane < MLA_ROPE_DIM, sin, 0.0))

    def rope(g):
        return g * t1 + pltpu.roll(g, MLA_ROPE_DIM, axis=1) * t2

    kr_ref[...] = rope(_dot(h, w_kr_ref[...])).astype(BF16)

    cq = _rmsnorm(proj(_C_CQ, _C_CKV), g_cq_ref[...]).astype(BF16)
    for hd in range(MLA_HEADS):
        c0 = hd * MLA_QK_PAD
        qh = _dot(cq, w_uq_ref[:, c0:c0 + MLA_QK_PAD])
        qm_ref[:, c0:c0 + LANES] = (qh[:, :LANES] * mla_scale).astype(BF16)
        qm_ref[:, c0 + LANES:c0 + MLA_QK_PAD] = (rope(qh[:, LANES:]) * mla_scale).astype(BF16)

    ckv = _rmsnorm(proj(_C_CKV, _C_KR), g_ckv_ref[...]).astype(BF16)
    kn_ref[...] = _dot(ckv, w_ukv_ref[:, :MLA_WIDTH]).astype(BF16)
    vm_ref[...] = _dot(ckv, w_ukv_ref[:, MLA_WIDTH:]).astype(BF16)


def _proj(x2, pos2, invf, g_pre, w_in_b, w_kr_p, g_cq, w_uq_p, g_ckv, w_ukv_p):
    n = x2.shape[0]
    tm = PROJ_ROWS
    row = lambda w: pl.BlockSpec((tm, w), lambda i: (i, 0))
    full = lambda a: pl.BlockSpec(a.shape, lambda i: (0, 0))
    out_widths = (SB_WIDTH, SB_WIDTH, SB_WIDTH, MLA_HEADS * MLA_QK_PAD, MLA_WIDTH, LANES, MLA_WIDTH)
    return pl.pallas_call(
        _proj_kernel,
        grid=(n // tm,),
        in_specs=[row(D_MODEL), row(1), full(invf), full(g_pre), full(w_in_b), full(w_kr_p), full(g_cq),
                  full(w_uq_p), full(g_ckv), full(w_ukv_p)],
        out_specs=[row(w) for w in out_widths],
        out_shape=[jax.ShapeDtypeStruct((n, w), BF16) for w in out_widths],
        compiler_params=pltpu.CompilerParams(dimension_semantics=("arbitrary",), vmem_limit_bytes=VMEM_LIMIT),
        name="proj",
    )(x2, pos2, invf, g_pre, w_in_b, w_kr_p, g_cq, w_uq_p, g_ckv, w_ukv_p)


def _softplus2(z):
    return jnp.maximum(z, 0.0) + jnp.log2(1.0 + jnp.exp2(-jnp.abs(z)))


def _sb_kernel(q_ref, k_ref, v_ref, o_ref):
    t = ATTN_TILE
    row = lax.broadcasted_iota(jnp.int32, (t, t), 0)
    col = lax.broadcasted_iota(jnp.int32, (t, t), 1)
    mask = col < row
    tri = (row >= col).astype(BF16)
    tri2 = jnp.concatenate([tri, tri], axis=0)

    for qi in range(q_ref.shape[0] // t):
        n = qi + 1
        z = _dot_nt(q_ref[qi * t:n * t, :], k_ref[0:n * t, :])
        carry = None
        a_tiles = [None] * n
        for j in range(qi, -1, -1):
            zj = z[:, j * t:(j + 1) * t]
            sp = _softplus2(zj)
            if j == qi:
                sp = jnp.where(mask, sp, 0.0)
            hi = sp.astype(BF16)
            lo = (sp - hi.astype(F32)).astype(BF16)
            local = _dot(jnp.concatenate([hi, lo], axis=1), tri2)
            cum = local if carry is None else local + carry
            a = jnp.exp2(zj - cum)
            if j == qi:
                a = jnp.where(mask, a, 0.0)
            a_tiles[j] = a.astype(BF16)
            carry = local[:, 0:1] if carry is None else carry + local[:, 0:1]
        a_all = a_tiles[0] if n == 1 else jnp.concatenate(a_tiles, axis=1)
        o_ref[qi * t:n * t, :] = _dot(a_all, v_ref[0:n * t, :])


def _sb_attention(q, k, v):
    b, s, _ = q.shape
    head = pl.BlockSpec((None, s, SB_HEAD_DIM), lambda bi, h: (bi, 0, h))
    return pl.pallas_call(
        _sb_kernel,
        grid=(b, SB_HEADS),
        in_specs=[head, head, head],
        out_specs=head,
        out_shape=jax.ShapeDtypeStruct((b, s, SB_WIDTH), F32),
        compiler_params=pltpu.CompilerParams(dimension_semantics=("arbitrary",) * 2, vmem_limit_bytes=VMEM_LIMIT),
        name="sb_attn",
    )(q, k, v)


def _mla_kernel(q_ref, kn_ref, kr_ref, v_ref, o_ref):
    t = ATTN_TILE
    row = lax.broadcasted_iota(jnp.int32, (t, t), 0)
    col = lax.broadcasted_iota(jnp.int32, (t, t), 1)
    mask = (col // CHUNK) <= (row // CHUNK)

    for qi in range(q_ref.shape[0] // t):
        n = qi + 1
        kcat = jnp.concatenate([kn_ref[0:n * t, :], kr_ref[0:n * t, :]], axis=1)
        sc = _dot_nt(q_ref[qi * t:n * t, :], kcat)
        diag = jnp.where(mask, sc[:, qi * t:], -jnp.inf)
        sc = diag if n == 1 else jnp.concatenate([sc[:, :qi * t], diag], axis=1)
        p = jnp.exp2(sc - jnp.max(sc, axis=-1, keepdims=True))
        vcat = jnp.concatenate([v_ref[0:n * t, :], jnp.ones((n * t, LANES), BF16)], axis=1)
        acc = _dot(p.astype(BF16), vcat)
        o_ref[qi * t:n * t, :] = acc[:, :MLA_V_DIM] / acc[:, MLA_V_DIM:]


def _mla_attention(q, kn, kr, v):
    b, s, _ = kn.shape
    head = lambda w: pl.BlockSpec((None, s, w), lambda bi, h: (bi, 0, h))
    return pl.pallas_call(
        _mla_kernel,
        grid=(b, MLA_HEADS),
        in_specs=[head(MLA_QK_PAD), head(MLA_NOPE_DIM), pl.BlockSpec((None, s, LANES), lambda bi, h: (bi, 0, 0)),
                  head(MLA_V_DIM)],
        out_specs=head(MLA_V_DIM),
        out_shape=jax.ShapeDtypeStruct((b, s, MLA_WIDTH), F32),
        compiler_params=pltpu.CompilerParams(dimension_semantics=("arbitrary",) * 2, vmem_limit_bytes=VMEM_LIMIT),
        name="mla_attn",
    )(q, kn, kr, v)


def _out_kernel(osb_ref, omla_ref, x_ref, g_sb_ref, g_mla_ref, w_o_ref, g_post_ref, g_ffn_ref, x1_ref, h2_ref):
    n_sb = _rmsnorm(osb_ref[...], g_sb_ref[...]).astype(BF16)
    n_mla = _rmsnorm(omla_ref[...], g_mla_ref[...]).astype(BF16)
    y = _dot(n_sb, w_o_ref[:SB_WIDTH, :]) + _dot(n_mla, w_o_ref[SB_WIDTH:, :])
    x1 = x_ref[...] + _rmsnorm(y, g_post_ref[...])
    x1_ref[...] = x1
    h2_ref[...] = _rmsnorm(x1, g_ffn_ref[...]).astype(BF16)


def _out_proj(o_sb, o_mla, x2, g_sb, g_mla, w_o, g_post, g_ffn):
    n = x2.shape[0]
    tm = OUT_ROWS
    row = lambda w: pl.BlockSpec((tm, w), lambda i: (i, 0))
    full = lambda a: pl.BlockSpec(a.shape, lambda i: (0, 0))
    return pl.pallas_call(
        _out_kernel,
        grid=(n // tm,),
        in_specs=[row(SB_WIDTH), row(MLA_WIDTH), row(D_MODEL), full(g_sb), full(g_mla), full(w_o), full(g_post),
                  full(g_ffn)],
        out_specs=[row(D_MODEL), row(D_MODEL)],
        out_shape=[jax.ShapeDtypeStruct((n, D_MODEL), F32), jax.ShapeDtypeStruct((n, D_MODEL), BF16)],
        compiler_params=pltpu.CompilerParams(dimension_semantics=("arbitrary",), vmem_limit_bytes=VMEM_LIMIT),
        name="out_proj",
    )(o_sb, o_mla, x2, g_sb, g_mla, w_o, g_post, g_ffn)


def _ffn_kernel(seq_tiles, h_ref, halo_ref, x1_ref, wg_ref, wv_ref, cwg_ref, cwv_ref, cbg_ref, cbv_ref, wd_ref,
                g_post_ref, o_ref, hcat_ref, p_ref, acc_ref):
    tm = FFN_ROWS
    i = pl.program_id(0)
    j = pl.program_id(1)

    @pl.when(j == 0)
    def _():
        first = (i % seq_tiles) == 0
        hcat_ref[0:HALO, :] = jnp.where(first, jnp.zeros_like(halo_ref[...]), halo_ref[...])
        hcat_ref[HALO:, :] = h_ref[...]

    def conv(w_ref, cw_ref, cb_ref):
        p_ref[...] = _dot(hcat_ref[...], w_ref[...])
        u = cb_ref[...] + cw_ref[CONV_WIDTH - 1:CONV_WIDTH, :] * p_ref[HALO:HALO + tm, :]
        for tap in range(1, CONV_WIDTH):
            k = CONV_WIDTH - 1 - tap
            u = u + cw_ref[k:k + 1, :] * p_ref[HALO - tap:HALO - tap + tm, :]
        return u

    gate = conv(wg_ref, cwg_ref, cbg_ref)
    val = conv(wv_ref, cwv_ref, cbv_ref)
    act = (jax.nn.gelu(gate, approximate=True) * val).astype(BF16)
    contrib = _dot(act, wd_ref[...])

    @pl.when(j == 0)
    def _():
        acc_ref[...] = contrib

    @pl.when(j > 0)
    def _():
        acc_ref[...] += contrib

    @pl.when(j == pl.num_programs(1) - 1)
    def _():
        o_ref[...] = x1_ref[...] + _rmsnorm(acc_ref[...], g_post_ref[...])


def _ffn(h2, x1, w_up, conv_w, conv_b, w_down, g_post, seq_len):
    n = h2.shape[0]
    tm, tn = FFN_ROWS, FFN_COLS
    nj = D_FF // tn
    halo_blocks = tm // HALO
    return pl.pallas_call(
        functools.partial(_ffn_kernel, seq_len // tm),
        grid=(n // tm, nj),
        in_specs=[pl.BlockSpec((tm, D_MODEL), lambda i, j: (i, 0)),
                  pl.BlockSpec((HALO, D_MODEL), lambda i, j: (jnp.maximum(i * halo_blocks - 1, 0), 0)),
                  pl.BlockSpec((tm, D_MODEL), lambda i, j: (i, 0)),
                  pl.BlockSpec((D_MODEL, tn), lambda i, j: (0, j)),
                  pl.BlockSpec((D_MODEL, tn), lambda i, j: (0, j + nj)),
                  pl.BlockSpec((CONV_WIDTH, tn), lambda i, j: (0, j)),
                  pl.BlockSpec((CONV_WIDTH, tn), lambda i, j: (0, j + nj)),
                  pl.BlockSpec((1, tn), lambda i, j: (0, j)),
                  pl.BlockSpec((1, tn), lambda i, j: (0, j + nj)),
                  pl.BlockSpec((tn, D_MODEL), lambda i, j: (j, 0)),
                  pl.BlockSpec((1, D_MODEL), lambda i, j: (0, 0))],
        out_specs=pl.BlockSpec((tm, D_MODEL), lambda i, j: (i, 0)),
        out_shape=jax.ShapeDtypeStruct((n, D_MODEL), F32),
        scratch_shapes=[pltpu.VMEM((tm + HALO, D_MODEL), BF16), pltpu.VMEM((tm + HALO, tn), F32),
                        pltpu.VMEM((tm, D_MODEL), F32)],
        compiler_params=pltpu.CompilerParams(dimension_semantics=("arbitrary", "arbitrary"),
                                             vmem_limit_bytes=VMEM_LIMIT),
        name="ffn",
    )(h2, h2, x1, w_up, w_up, conv_w, conv_w, conv_b, conv_b, w_down, g_post)


def _swap_halves(w):
    half = w.shape[-1] // 2
    return jnp.concatenate([w[..., half:], w[..., :half]], axis=-1)


def _layout_w_kr(w_in):
    k_rope = w_in[:, _C_KR:_C_KR + MLA_ROPE_DIM]
    return jnp.concatenate([k_rope, _swap_halves(k_rope)], axis=1).astype(BF16)


def _layout_w_uq(w_uq):
    w = w_uq.reshape(MLA_Q_RANK, MLA_HEADS, MLA_QK_DIM)
    rope = w[:, :, MLA_NOPE_DIM:]
    w = jnp.concatenate([w[:, :, :MLA_NOPE_DIM], rope, _swap_halves(rope)], axis=-1)
    return w.reshape(MLA_Q_RANK, MLA_HEADS * MLA_QK_PAD).astype(BF16)


def _layout_w_ukv(w_ukv):
    w = w_ukv.reshape(MLA_KV_RANK, MLA_HEADS, MLA_NOPE_DIM + MLA_V_DIM)
    k_nope = w[:, :, :MLA_NOPE_DIM].reshape(MLA_KV_RANK, MLA_WIDTH)
    v = w[:, :, MLA_NOPE_DIM:].reshape(MLA_KV_RANK, MLA_WIDTH)
    return jnp.concatenate([k_nope, v], axis=1).astype(BF16)


def kernel(x, positions, g_attn_pre, w_in, g_cq, w_uq, g_ckv, w_ukv, g_out_sb, g_out_mla, w_o, g_attn_post,
           g_ffn_pre, w_up, conv_w, conv_b, w_down, g_ffn_post):
    b, s, d = x.shape
    depth = w_in.shape[0]
    n = b * s
    half = MLA_ROPE_DIM // 2
    inv_freq = ROPE_THETA ** (-jnp.arange(half, dtype=F32) / half)
    invf = jnp.tile(inv_freq, LANES // half)[None, :]
    pos2 = positions.reshape(n, 1)
    x2 = x.reshape(n, d)
    r2 = lambda g: g[None, :]
    for l in range(depth):
        qsb, ksb, vsb, qm, kn, kr, vm = _proj(
            x2, pos2, invf, r2(g_attn_pre[l]), w_in[l].astype(BF16), _layout_w_kr(w_in[l]), r2(g_cq[l]),
            _layout_w_uq(w_uq[l]), r2(g_ckv[l]), _layout_w_ukv(w_ukv[l]))
        b3 = lambda a: a.reshape(b, s, a.shape[-1])
        o_sb = _sb_attention(b3(qsb), b3(ksb), b3(vsb))
        o_mla = _mla_attention(b3(qm), b3(kn), b3(kr), b3(vm))
        x1, h2 = _out_proj(o_sb.reshape(n, SB_WIDTH), o_mla.reshape(n, MLA_WIDTH), x2, r2(g_out_sb[l]),
                           r2(g_out_mla[l]), w_o[l].astype(BF16), r2(g_attn_post[l]), r2(g_ffn_pre[l]))
        x2 = _ffn(h2, x1, w_up[l].astype(BF16), conv_w[l], r2(conv_b[l]), w_down[l].astype(BF16),
                  r2(g_ffn_post[l]), s)
    return x2.reshape(b, s, d)
```

```python
import functools

import jax
import jax.numpy as jnp
from jax import lax
from jax.experimental import pallas as pl
from jax.experimental.pallas import tpu as pltpu

F32 = jnp.float32
BF16 = jnp.bfloat16

D_MODEL = 2048
CHUNK = 64
SB_HEADS = 8
SB_HEAD_DIM = 128
SB_WIDTH = SB_HEADS * SB_HEAD_DIM
MLA_HEADS = 8
MLA_NOPE_DIM = 128
MLA_ROPE_DIM = 64
MLA_V_DIM = 128
MLA_Q_RANK = 512
MLA_KV_RANK = 256
MLA_QK_DIM = MLA_NOPE_DIM + MLA_ROPE_DIM
MLA_WIDTH = MLA_HEADS * MLA_V_DIM
D_FF = 5632
CONV_WIDTH = 3
ROPE_THETA = 10000.0
EPS = 1e-6
LOG2_E = 1.4426950408889634

LANES = 128
BF16_SUBLANES = 16
MLA_QK_PAD = 2 * LANES

_C_QSB = 0
_C_KSB = SB_WIDTH
_C_VSB = 2 * SB_WIDTH
_C_CQ = 3 * SB_WIDTH
_C_CKV = _C_CQ + MLA_Q_RANK
_C_KR = _C_CKV + MLA_KV_RANK

PROJ_ROWS = 512
ATTN_TILE = 256
OUT_ROWS = 512
FFN_ROWS = 512
FFN_COLS = 512
HALO = BF16_SUBLANES

VMEM_LIMIT = 56 * 1024 * 1024


def _rmsnorm(x, g):
    return x * lax.rsqrt(jnp.mean(x * x, axis=-1, keepdims=True) + EPS) * g


def _resident(a):
    return pl.BlockSpec(a.shape, lambda *_: (0,) * a.ndim, pipeline_mode=pl.Buffered(1))


def _dot(a, b):
    return jnp.dot(a, b, preferred_element_type=F32)


def _dot_nt(a, b):
    return lax.dot_general(a, b, (((1,), (1,)), ((), ())), preferred_element_type=F32)


def _proj_kernel(x_ref, pos_ref, invf_ref, g_pre_ref, w_in_ref, w_kr_ref, g_cq_ref, w_uq_ref, g_ckv_ref, w_ukv_ref,
                 qsb_ref, ksb_ref, vsb_ref, qm_ref, kn_ref, kr_ref, vm_ref):
    sb_scale = SB_HEAD_DIM ** -0.5 * LOG2_E
    mla_scale = MLA_QK_DIM ** -0.5 * LOG2_E
    h = _rmsnorm(x_ref[...], g_pre_ref[...]).astype(BF16)

    def proj(lo, hi):
        return _dot(h, w_in_ref[:, lo:hi])

    cq = proj(_C_CQ, _C_CKV)
    ckv = proj(_C_CKV, _C_KR)
    kr = _dot(h, w_kr_ref[...])
    qsb_ref[...] = (proj(_C_QSB, _C_KSB) * sb_scale).astype(BF16)
    ksb_ref[...] = proj(_C_KSB, _C_VSB).astype(BF16)
    vsb_ref[...] = proj(_C_VSB, _C_CQ).astype(BF16)

    ang = pos_ref[...].astype(F32) * invf_ref[...]
    lane = lax.broadcasted_iota(jnp.int32, ang.shape, 1)
    cos = jnp.cos(ang)
    sin = jnp.sin(ang)
    t1 = jnp.where(lane < MLA_ROPE_DIM, cos, 0.0)
    t2 = jnp.where(lane < MLA_ROPE_DIM // 2, -sin, jnp.where(lane < MLA_ROPE_DIM, sin, 0.0))

    def rope(g):
        return g * t1 + pltpu.roll(g, MLA_ROPE_DIM, axis=1) * t2

    kr_ref[...] = rope(kr).astype(BF16)

    cq = _rmsnorm(cq, g_cq_ref[...]).astype(BF16)
    for hd in range(MLA_HEADS):
        c0 = hd * MLA_QK_PAD
        qh = _dot(cq, w_uq_ref[:, c0:c0 + MLA_QK_PAD])
        qm_ref[:, c0:c0 + LANES] = (qh[:, :LANES] * mla_scale).astype(BF16)
        qm_ref[:, c0 + LANES:c0 + MLA_QK_PAD] = (rope(qh[:, LANES:]) * mla_scale).astype(BF16)

    ckv = _rmsnorm(ckv, g_ckv_ref[...]).astype(BF16)
    kn_ref[...] = _dot(ckv, w_ukv_ref[:, :MLA_WIDTH]).astype(BF16)
    vm_ref[...] = _dot(ckv, w_ukv_ref[:, MLA_WIDTH:]).astype(BF16)


def _proj(x2, pos2, invf, g_pre, w_in_b, w_kr_p, g_cq, w_uq_p, g_ckv, w_ukv_p):
    n = x2.shape[0]
    tm = PROJ_ROWS
    row = lambda w: pl.BlockSpec((tm, w), lambda i: (i, 0))
    full = _resident
    out_widths = (SB_WIDTH, SB_WIDTH, SB_WIDTH, MLA_HEADS * MLA_QK_PAD, MLA_WIDTH, LANES, MLA_WIDTH)
    return pl.pallas_call(
        _proj_kernel,
        grid=(n // tm,),
        in_specs=[row(D_MODEL), row(1), full(invf), full(g_pre), full(w_in_b), full(w_kr_p), full(g_cq),
                  full(w_uq_p), full(g_ckv), full(w_ukv_p)],
        out_specs=[row(w) for w in out_widths],
        out_shape=[jax.ShapeDtypeStruct((n, w), BF16) for w in out_widths],
        compiler_params=pltpu.CompilerParams(dimension_semantics=("arbitrary",), vmem_limit_bytes=VMEM_LIMIT),
        name="proj",
    )(x2, pos2, invf, g_pre, w_in_b, w_kr_p, g_cq, w_uq_p, g_ckv, w_ukv_p)


def _softplus2(z):
    return jnp.maximum(z, 0.0) + jnp.log2(1.0 + jnp.exp2(-jnp.abs(z)))


def _cast_slabs(src_refs, dst_refs):
    for src_ref, dst_ref in zip(src_refs, dst_refs):
        dst_ref[...] = src_ref[...].astype(BF16)


def _slab_specs(weights, batch, heads):
    return [pl.BlockSpec((w.shape[0] // (batch * heads), w.shape[1]), lambda bi, h: (bi * heads + h, 0))
            for w in weights]


def _sb_kernel(n_cast, q_ref, k_ref, v_ref, *refs):
    o_ref = refs[n_cast]
    _cast_slabs(refs[:n_cast], refs[n_cast + 1:])
    t = ATTN_TILE
    row = lax.broadcasted_iota(jnp.int32, (t, t), 0)
    col = lax.broadcasted_iota(jnp.int32, (t, t), 1)
    mask = col < row
    tri = (row >= col).astype(BF16)
    tri2 = jnp.concatenate([tri, tri], axis=0)

    for qi in range(q_ref.shape[0] // t):
        n = qi + 1
        z = _dot_nt(q_ref[qi * t:n * t, :], k_ref[0:n * t, :])
        carry = None
        a_tiles = [None] * n
        for j in range(qi, -1, -1):
            zj = z[:, j * t:(j + 1) * t]
            sp = _softplus2(zj)
            if j == qi:
                sp = jnp.where(mask, sp, 0.0)
            hi = sp.astype(BF16)
            lo = (sp - hi.astype(F32)).astype(BF16)
            local = _dot(jnp.concatenate([hi, lo], axis=1), tri2)
            cum = local if carry is None else local + carry
            a = jnp.exp2(zj - cum)
            if j == qi:
                a = jnp.where(mask, a, 0.0)
            a_tiles[j] = a.astype(BF16)
            carry = local[:, 0:1] if carry is None else carry + local[:, 0:1]
        a_all = a_tiles[0] if n == 1 else jnp.concatenate(a_tiles, axis=1)
        o_ref[qi * t:n * t, :] = _dot(a_all, v_ref[0:n * t, :])


def _sb_attention(q, k, v, weights):
    b, s, _ = q.shape
    head = pl.BlockSpec((None, s, SB_HEAD_DIM), lambda bi, h: (bi, 0, h))
    slabs = _slab_specs(weights, b, SB_HEADS)
    return pl.pallas_call(
        functools.partial(_sb_kernel, len(weights)),
        grid=(b, SB_HEADS),
        in_specs=[head, head, head] + slabs,
        out_specs=[head] + slabs,
        out_shape=[jax.ShapeDtypeStruct((b, s, SB_WIDTH), F32)] +
                  [jax.ShapeDtypeStruct(w.shape, BF16) for w in weights],
        compiler_params=pltpu.CompilerParams(dimension_semantics=("arbitrary",) * 2, vmem_limit_bytes=VMEM_LIMIT),
        name="sb_attn",
    )(q, k, v, *weights)


def _mla_kernel(n_cast, q_ref, kn_ref, kr_ref, v_ref, *refs):
    o_ref = refs[n_cast]
    _cast_slabs(refs[:n_cast], refs[n_cast + 1:])
    t = ATTN_TILE
    row = lax.broadcasted_iota(jnp.int32, (t, t), 0)
    col = lax.broadcasted_iota(jnp.int32, (t, t), 1)
    mask = (col // CHUNK) <= (row // CHUNK)

    for qi in range(q_ref.shape[0] // t):
        n = qi + 1
        kcat = jnp.concatenate([kn_ref[0:n * t, :], kr_ref[0:n * t, :]], axis=1)
        sc = _dot_nt(q_ref[qi * t:n * t, :], kcat)
        diag = jnp.where(mask, sc[:, qi * t:], -jnp.inf)
        sc = diag if n == 1 else jnp.concatenate([sc[:, :qi * t], diag], axis=1)
        p = jnp.exp2(sc - jnp.max(sc, axis=-1, keepdims=True))
        vcat = jnp.concatenate([v_ref[0:n * t, :], jnp.ones((n * t, LANES), BF16)], axis=1)
        acc = _dot(p.astype(BF16), vcat)
        o_ref[qi * t:n * t, :] = acc[:, :MLA_V_DIM] / acc[:, MLA_V_DIM:]


def _mla_attention(q, kn, kr, v, weights):
    b, s, _ = kn.shape
    head = lambda w: pl.BlockSpec((None, s, w), lambda bi, h: (bi, 0, h))
    slabs = _slab_specs(weights, b, MLA_HEADS)
    return pl.pallas_call(
        functools.partial(_mla_kernel, len(weights)),
        grid=(b, MLA_HEADS),
        in_specs=[head(MLA_QK_PAD), head(MLA_NOPE_DIM), pl.BlockSpec((None, s, LANES), lambda bi, h: (bi, 0, 0)),
                  head(MLA_V_DIM)] + slabs,
        out_specs=[head(MLA_V_DIM)] + slabs,
        out_shape=[jax.ShapeDtypeStruct((b, s, MLA_WIDTH), F32)] +
                  [jax.ShapeDtypeStruct(w.shape, BF16) for w in weights],
        compiler_params=pltpu.CompilerParams(dimension_semantics=("arbitrary",) * 2, vmem_limit_bytes=VMEM_LIMIT),
        name="mla_attn",
    )(q, kn, kr, v, *weights)


def _out_kernel(osb_ref, omla_ref, x_ref, g_sb_ref, g_mla_ref, w_o_ref, g_post_ref, g_ffn_ref, x1_ref, h2_ref):
    n_sb = _rmsnorm(osb_ref[...], g_sb_ref[...]).astype(BF16)
    n_mla = _rmsnorm(omla_ref[...], g_mla_ref[...]).astype(BF16)
    y = _dot(n_sb, w_o_ref[:SB_WIDTH, :]) + _dot(n_mla, w_o_ref[SB_WIDTH:, :])
    x1 = x_ref[...] + _rmsnorm(y, g_post_ref[...])
    x1_ref[...] = x1
    h2_ref[...] = _rmsnorm(x1, g_ffn_ref[...]).astype(BF16)


def _out_proj(o_sb, o_mla, x2, g_sb, g_mla, w_o, g_post, g_ffn):
    n = x2.shape[0]
    tm = OUT_ROWS
    row = lambda w: pl.BlockSpec((tm, w), lambda i: (i, 0))
    full = _resident
    return pl.pallas_call(
        _out_kernel,
        grid=(n // tm,),
        in_specs=[row(SB_WIDTH), row(MLA_WIDTH), row(D_MODEL), full(g_sb), full(g_mla), full(w_o), full(g_post),
                  full(g_ffn)],
        out_specs=[row(D_MODEL), row(D_MODEL)],
        out_shape=[jax.ShapeDtypeStruct((n, D_MODEL), F32), jax.ShapeDtypeStruct((n, D_MODEL), BF16)],
        compiler_params=pltpu.CompilerParams(dimension_semantics=("arbitrary",), vmem_limit_bytes=VMEM_LIMIT),
        name="out_proj",
    )(o_sb, o_mla, x2, g_sb, g_mla, w_o, g_post, g_ffn)


def _ffn_kernel(seq_tiles, h_ref, halo_ref, x1_ref, wg_ref, wv_ref, cwg_ref, cwv_ref, cbg_ref, cbv_ref, wd_ref,
                g_post_ref, o_ref, hcat_ref, pg_ref, pv_ref, act_ref, acc_ref):
    tm = FFN_ROWS
    i = pl.program_id(0)
    j = pl.program_id(1)
    last = pl.num_programs(1) - 1

    def conv(p_ref, w_ref, cw_ref, cb_ref):
        p_ref[...] = _dot(hcat_ref[...], w_ref[...])
        u = cb_ref[...] + cw_ref[CONV_WIDTH - 1:CONV_WIDTH, :] * p_ref[HALO:HALO + tm, :]
        for tap in range(1, CONV_WIDTH):
            k = CONV_WIDTH - 1 - tap
            u = u + cw_ref[k:k + 1, :] * p_ref[HALO - tap:HALO - tap + tm, :]
        return u

    def build(slot):
        gate = conv(pg_ref, wg_ref, cwg_ref, cbg_ref)
        val = conv(pv_ref, wv_ref, cwv_ref, cbv_ref)
        act_ref[slot] = (jax.nn.gelu(gate, approximate=True) * val).astype(BF16)

    def multiply(slot):
        return _dot(act_ref[slot], wd_ref[...])

    @pl.when(j == 0)
    def _():
        first = (i % seq_tiles) == 0
        hcat_ref[0:HALO, :] = jnp.where(first, jnp.zeros_like(halo_ref[...]), halo_ref[...])
        hcat_ref[HALO:, :] = h_ref[...]
        acc_ref[...] = jnp.zeros_like(acc_ref)
        build(0)

    @pl.when(jnp.logical_and(j > 0, j < last))
    def _():
        slot = j % 2
        acc_ref[...] += multiply(1 - slot)
        build(slot)

    @pl.when(j == last)
    def _():
        y = acc_ref[...] + multiply((last - 1) % 2)
        o_ref[...] = x1_ref[...] + _rmsnorm(y, g_post_ref[...])


def _ffn(h2, x1, w_up, conv_w, conv_b, w_down, g_post, seq_len):
    n = h2.shape[0]
    tm, tn = FFN_ROWS, FFN_COLS
    nj = D_FF // tn
    halo_blocks = tm // HALO
    up = lambda i, j: (0, jnp.minimum(j, nj - 1))
    up_val = lambda i, j: (0, jnp.minimum(j, nj - 1) + nj)
    return pl.pallas_call(
        functools.partial(_ffn_kernel, seq_len // tm),
        grid=(n // tm, nj + 1),
        in_specs=[pl.BlockSpec((tm, D_MODEL), lambda i, j: (i, 0)),
                  pl.BlockSpec((HALO, D_MODEL), lambda i, j: (jnp.maximum(i * halo_blocks - 1, 0), 0)),
                  pl.BlockSpec((tm, D_MODEL), lambda i, j: (i, 0)),
                  pl.BlockSpec((D_MODEL, tn), up),
                  pl.BlockSpec((D_MODEL, tn), up_val),
                  pl.BlockSpec((CONV_WIDTH, tn), up),
                  pl.BlockSpec((CONV_WIDTH, tn), up_val),
                  pl.BlockSpec((1, tn), up),
                  pl.BlockSpec((1, tn), up_val),
                  pl.BlockSpec((tn, D_MODEL), lambda i, j: (jnp.maximum(j - 1, 0), 0)),
                  pl.BlockSpec((1, D_MODEL), lambda i, j: (0, 0))],
        out_specs=pl.BlockSpec((tm, D_MODEL), lambda i, j: (i, 0)),
        out_shape=jax.ShapeDtypeStruct((n, D_MODEL), F32),
        scratch_shapes=[pltpu.VMEM((tm + HALO, D_MODEL), BF16), pltpu.VMEM((tm + HALO, tn), F32),
                        pltpu.VMEM((tm + HALO, tn), F32), pltpu.VMEM((2, tm, tn), BF16),
                        pltpu.VMEM((tm, D_MODEL), F32)],
        compiler_params=pltpu.CompilerParams(dimension_semantics=("arbitrary", "arbitrary"),
                                             vmem_limit_bytes=VMEM_LIMIT),
        name="ffn",
    )(h2, h2, x1, w_up, w_up, conv_w, conv_w, conv_b, conv_b, w_down, g_post)


def _swap_halves(w):
    half = w.shape[-1] // 2
    return jnp.concatenate([w[..., half:], w[..., :half]], axis=-1)


def _layout_w_kr(w_in):
    k_rope = w_in[:, _C_KR:_C_KR + MLA_ROPE_DIM]
    return jnp.concatenate([k_rope, _swap_halves(k_rope)], axis=1).astype(BF16)


def _layout_w_uq(w_uq):
    w = w_uq.reshape(MLA_Q_RANK, MLA_HEADS, MLA_QK_DIM)
    rope = w[:, :, MLA_NOPE_DIM:]
    w = jnp.concatenate([w[:, :, :MLA_NOPE_DIM], rope, _swap_halves(rope)], axis=-1)
    return w.reshape(MLA_Q_RANK, MLA_HEADS * MLA_QK_PAD).astype(BF16)


def _layout_w_ukv(w_ukv):
    w = w_ukv.reshape(MLA_KV_RANK, MLA_HEADS, MLA_NOPE_DIM + MLA_V_DIM)
    k_nope = w[:, :, :MLA_NOPE_DIM].reshape(MLA_KV_RANK, MLA_WIDTH)
    v = w[:, :, MLA_NOPE_DIM:].reshape(MLA_KV_RANK, MLA_WIDTH)
    return jnp.concatenate([k_nope, v], axis=1).astype(BF16)


def kernel(x, positions, g_attn_pre, w_in, g_cq, w_uq, g_ckv, w_ukv, g_out_sb, g_out_mla, w_o, g_attn_post,
           g_ffn_pre, w_up, conv_w, conv_b, w_down, g_ffn_post):
    b, s, d = x.shape
    depth = w_in.shape[0]
    n = b * s
    half = MLA_ROPE_DIM // 2
    inv_freq = ROPE_THETA ** (-jnp.arange(half, dtype=F32) / half)
    invf = jnp.tile(inv_freq, LANES // half)[None, :]
    pos2 = positions.reshape(n, 1)
    x2 = x.reshape(n, d)
    r2 = lambda g: g[None, :]
    for l in range(depth):
        qsb, ksb, vsb, qm, kn, kr, vm = _proj(
            x2, pos2, invf, r2(g_attn_pre[l]), w_in[l].astype(BF16), _layout_w_kr(w_in[l]), r2(g_cq[l]),
            _layout_w_uq(w_uq[l]), r2(g_ckv[l]), _layout_w_ukv(w_ukv[l]))
        b3 = lambda a: a.reshape(b, s, a.shape[-1])
        o_sb, w_up_b = _sb_attention(b3(qsb), b3(ksb), b3(vsb), [w_up[l]])
        o_mla, w_o_b, w_down_b = _mla_attention(b3(qm), b3(kn), b3(kr), b3(vm), [w_o[l], w_down[l]])
        x1, h2 = _out_proj(o_sb.reshape(n, SB_WIDTH), o_mla.reshape(n, MLA_WIDTH), x2, r2(g_out_sb[l]),
                           r2(g_out_mla[l]), w_o_b, r2(g_attn_post[l]), r2(g_ffn_pre[l]))
        x2 = _ffn(h2, x1, w_up_b, conv_w[l], r2(conv_b[l]), w_down_b, r2(g_ffn_post[l]), s)
    return x2.reshape(b, s, d)
```

```python
import functools

import jax
import jax.numpy as jnp
from jax import lax
from jax.experimental import pallas as pl
from jax.experimental.pallas import tpu as pltpu

F32 = jnp.float32
BF16 = jnp.bfloat16

D_MODEL = 2048
CHUNK = 64
SB_HEADS = 8
SB_HEAD_DIM = 128
SB_WIDTH = SB_HEADS * SB_HEAD_DIM
MLA_HEADS = 8
MLA_NOPE_DIM = 128
MLA_ROPE_DIM = 64
MLA_V_DIM = 128
MLA_Q_RANK = 512
MLA_KV_RANK = 256
MLA_QK_DIM = MLA_NOPE_DIM + MLA_ROPE_DIM
MLA_WIDTH = MLA_HEADS * MLA_V_DIM
D_FF = 5632
CONV_WIDTH = 3
ROPE_THETA = 10000.0
EPS = 1e-6
LOG2_E = 1.4426950408889634

LANES = 128
BF16_SUBLANES = 16
MLA_QK_PAD = 2 * LANES

_C_QSB = 0
_C_KSB = SB_WIDTH
_C_VSB = 2 * SB_WIDTH
_C_CQ = 3 * SB_WIDTH
_C_CKV = _C_CQ + MLA_Q_RANK
_C_KR = _C_CKV + MLA_KV_RANK

PROJ_ROWS = 512
PROJ_CHUNK = 256
ATTN_TILE = 256
OUT_ROWS = 512
OUT_CHUNK = 256
FFN_ROWS = 512
FFN_COLS = 512
HALO = BF16_SUBLANES

VMEM_LIMIT = 56 * 1024 * 1024


def _rmsnorm(x, g):
    return x * lax.rsqrt(jnp.mean(x * x, axis=-1, keepdims=True) + EPS) * g


def _resident(a):
    return pl.BlockSpec(a.shape, lambda *_: (0,) * a.ndim, pipeline_mode=pl.Buffered(1))


def _dot(a, b):
    return jnp.dot(a, b, preferred_element_type=F32)


def _dot_nt(a, b):
    return lax.dot_general(a, b, (((1,), (1,)), ((), ())), preferred_element_type=F32)


def _proj_kernel(x_ref, pos_ref, invf_ref, g_pre_ref, w_in_ref, w_kr_ref, g_cq_ref, w_uq_ref, g_ckv_ref, w_ukv_ref,
                 qsb_ref, ksb_ref, vsb_ref, qm_ref, kn_ref, kr_ref, vm_ref):
    sb_scale = SB_HEAD_DIM ** -0.5 * LOG2_E
    mla_scale = MLA_QK_DIM ** -0.5 * LOG2_E
    def wide(rs):
        h = _rmsnorm(x_ref[rs, :], g_pre_ref[...]).astype(BF16)

        def proj(lo, hi):
            return _dot(h, w_in_ref[:, lo:hi])

        cq = proj(_C_CQ, _C_CKV)
        ckv = proj(_C_CKV, _C_KR)
        kr = _dot(h, w_kr_ref[...])
        qsb_ref[rs, :] = (proj(_C_QSB, _C_KSB) * sb_scale).astype(BF16)
        ksb_ref[rs, :] = proj(_C_KSB, _C_VSB).astype(BF16)
        vsb_ref[rs, :] = proj(_C_VSB, _C_CQ).astype(BF16)
        return cq, ckv, kr

    def latent(rs, cq, ckv, kr):
        ang = pos_ref[rs, :].astype(F32) * invf_ref[...]
        lane = lax.broadcasted_iota(jnp.int32, ang.shape, 1)
        cos = jnp.cos(ang)
        sin = jnp.sin(ang)
        t1 = jnp.where(lane < MLA_ROPE_DIM, cos, 0.0)
        t2 = jnp.where(lane < MLA_ROPE_DIM // 2, -sin, jnp.where(lane < MLA_ROPE_DIM, sin, 0.0))

        def rope(g):
            return g * t1 + pltpu.roll(g, MLA_ROPE_DIM, axis=1) * t2

        kr_ref[rs, :] = rope(kr).astype(BF16)

        cq = _rmsnorm(cq, g_cq_ref[...]).astype(BF16)
        for hd in range(MLA_HEADS):
            c0 = hd * MLA_QK_PAD
            qh = _dot(cq, w_uq_ref[:, c0:c0 + MLA_QK_PAD])
            qm_ref[rs, c0:c0 + LANES] = (qh[:, :LANES] * mla_scale).astype(BF16)
            qm_ref[rs, c0 + LANES:c0 + MLA_QK_PAD] = (rope(qh[:, LANES:]) * mla_scale).astype(BF16)

        ckv = _rmsnorm(ckv, g_ckv_ref[...]).astype(BF16)
        kn_ref[rs, :] = _dot(ckv, w_ukv_ref[:, :MLA_WIDTH]).astype(BF16)
        vm_ref[rs, :] = _dot(ckv, w_ukv_ref[:, MLA_WIDTH:]).astype(BF16)

    pending = None
    for r in range(0, x_ref.shape[0], PROJ_CHUNK):
        rs = slice(r, r + PROJ_CHUNK)
        latents = wide(rs)
        if pending is not None:
            latent(*pending)
        pending = (rs,) + latents
    latent(*pending)


def _proj(x2, pos2, invf, g_pre, w_in_b, w_kr_p, g_cq, w_uq_p, g_ckv, w_ukv_p):
    n = x2.shape[0]
    tm = PROJ_ROWS
    row = lambda w: pl.BlockSpec((tm, w), lambda i: (i, 0))
    full = _resident
    out_widths = (SB_WIDTH, SB_WIDTH, SB_WIDTH, MLA_HEADS * MLA_QK_PAD, MLA_WIDTH, LANES, MLA_WIDTH)
    return pl.pallas_call(
        _proj_kernel,
        grid=(n // tm,),
        in_specs=[row(D_MODEL), row(1), full(invf), full(g_pre), full(w_in_b), full(w_kr_p), full(g_cq),
                  full(w_uq_p), full(g_ckv), full(w_ukv_p)],
        out_specs=[row(w) for w in out_widths],
        out_shape=[jax.ShapeDtypeStruct((n, w), BF16) for w in out_widths],
        compiler_params=pltpu.CompilerParams(dimension_semantics=("arbitrary",), vmem_limit_bytes=VMEM_LIMIT),
        name="proj",
    )(x2, pos2, invf, g_pre, w_in_b, w_kr_p, g_cq, w_uq_p, g_ckv, w_ukv_p)


def _softplus2(z):
    return jnp.maximum(z, 0.0) + jnp.log2(1.0 + jnp.exp2(-jnp.abs(z)))


def _cast_slabs(src_refs, dst_refs):
    for src_ref, dst_ref in zip(src_refs, dst_refs):
        if len(dst_ref.shape) == 3:
            width = dst_ref.shape[2]
            for c in range(dst_ref.shape[0]):
                dst_ref[c] = src_ref[:, c * width:(c + 1) * width].astype(BF16)
        else:
            dst_ref[...] = src_ref[...].astype(BF16)


def _slab_specs(weights, col_blocks, batch, heads):
    in_specs, out_specs, out_shapes = [], [], []
    for w, width in zip(weights, col_blocks):
        rows = w.shape[0] // (batch * heads)
        in_specs.append(pl.BlockSpec((rows, w.shape[1]), lambda bi, h: (bi * heads + h, 0)))
        if width is None:
            out_specs.append(pl.BlockSpec((rows, w.shape[1]), lambda bi, h: (bi * heads + h, 0)))
            out_shapes.append(jax.ShapeDtypeStruct(w.shape, BF16))
        else:
            n_blocks = w.shape[1] // width
            out_specs.append(pl.BlockSpec((n_blocks, rows, width), lambda bi, h: (0, bi * heads + h, 0)))
            out_shapes.append(jax.ShapeDtypeStruct((n_blocks, w.shape[0], width), BF16))
    return in_specs, out_specs, out_shapes


def _sb_kernel(n_cast, q_ref, k_ref, v_ref, *refs):
    o_ref = refs[n_cast]
    _cast_slabs(refs[:n_cast], refs[n_cast + 1:])
    t = ATTN_TILE
    row = lax.broadcasted_iota(jnp.int32, (t, t), 0)
    col = lax.broadcasted_iota(jnp.int32, (t, t), 1)
    mask = col < row
    tri = (row >= col).astype(BF16)
    tri2 = jnp.concatenate([tri, tri], axis=0)

    n_tiles = q_ref.shape[0] // t

    def logits(qi):
        return _dot_nt(q_ref[qi * t:(qi + 1) * t, :], k_ref[0:(qi + 1) * t, :])

    def suffix_sums(zj, diag):
        sp = _softplus2(zj)
        if diag:
            sp = jnp.where(mask, sp, 0.0)
        hi = sp.astype(BF16)
        lo = (sp - hi.astype(F32)).astype(BF16)
        return _dot(jnp.concatenate([hi, lo], axis=1), tri2)

    z = [logits(0)] + [None] * (n_tiles - 1)
    a_tiles = [[None] * (qi + 1) for qi in range(n_tiles)]
    carry = [None] * n_tiles

    def finish(qi, j, zj, local):
        cum = local if carry[qi] is None else local + carry[qi]
        a = jnp.exp2(zj - cum)
        if j == qi:
            a = jnp.where(mask, a, 0.0)
        a_tiles[qi][j] = a.astype(BF16)
        carry[qi] = local[:, 0:1] if carry[qi] is None else carry[qi] + local[:, 0:1]
        if j == 0:
            a_all = a_tiles[qi][0] if qi == 0 else jnp.concatenate(a_tiles[qi], axis=1)
            o_ref[qi * t:(qi + 1) * t, :] = _dot(a_all, v_ref[0:(qi + 1) * t, :])

    pending = None
    for qi in range(n_tiles):
        for j in range(qi, -1, -1):
            if j == qi and qi + 1 < n_tiles:
                z[qi + 1] = logits(qi + 1)
            zj = z[qi][:, j * t:(j + 1) * t]
            local = suffix_sums(zj, j == qi)
            if pending is not None:
                finish(*pending)
            pending = (qi, j, zj, local)
    finish(*pending)


def _sb_attention(q, k, v, weights, col_blocks):
    b, s, _ = q.shape
    head = pl.BlockSpec((None, s, SB_HEAD_DIM), lambda bi, h: (bi, 0, h))
    slab_in, slab_out, slab_shapes = _slab_specs(weights, col_blocks, b, SB_HEADS)
    return pl.pallas_call(
        functools.partial(_sb_kernel, len(weights)),
        grid=(b, SB_HEADS),
        in_specs=[head, head, head] + slab_in,
        out_specs=[head] + slab_out,
        out_shape=[jax.ShapeDtypeStruct((b, s, SB_WIDTH), F32)] + slab_shapes,
        compiler_params=pltpu.CompilerParams(dimension_semantics=("arbitrary",) * 2, vmem_limit_bytes=VMEM_LIMIT),
        name="sb_attn",
    )(q, k, v, *weights)


def _mla_kernel(n_cast, q_ref, kn_ref, kr_ref, v_ref, *refs):
    o_ref = refs[n_cast]
    _cast_slabs(refs[:n_cast], refs[n_cast + 1:])
    t = ATTN_TILE
    row = lax.broadcasted_iota(jnp.int32, (t, t), 0)
    col = lax.broadcasted_iota(jnp.int32, (t, t), 1)
    mask = (col // CHUNK) <= (row // CHUNK)

    def scores(qi):
        n = qi + 1
        kcat = jnp.concatenate([kn_ref[0:n * t, :], kr_ref[0:n * t, :]], axis=1)
        return _dot_nt(q_ref[qi * t:n * t, :], kcat)

    n_tiles = q_ref.shape[0] // t
    sc_next = scores(0)
    for qi in range(n_tiles):
        n = qi + 1
        sc = sc_next
        if n < n_tiles:
            sc_next = scores(n)
        diag = jnp.where(mask, sc[:, qi * t:], -jnp.inf)
        sc = diag if n == 1 else jnp.concatenate([sc[:, :qi * t], diag], axis=1)
        p = jnp.exp2(sc - jnp.max(sc, axis=-1, keepdims=True))
        vcat = jnp.concatenate([v_ref[0:n * t, :], jnp.ones((n * t, LANES), BF16)], axis=1)
        acc = _dot(p.astype(BF16), vcat)
        o_ref[qi * t:n * t, :] = acc[:, :MLA_V_DIM] / acc[:, MLA_V_DIM:]


def _mla_attention(q, kn, kr, v, weights, col_blocks):
    b, s, _ = kn.shape
    head = lambda w: pl.BlockSpec((None, s, w), lambda bi, h: (bi, 0, h))
    slab_in, slab_out, slab_shapes = _slab_specs(weights, col_blocks, b, MLA_HEADS)
    return pl.pallas_call(
        functools.partial(_mla_kernel, len(weights)),
        grid=(b, MLA_HEADS),
        in_specs=[head(MLA_QK_PAD), head(MLA_NOPE_DIM), pl.BlockSpec((None, s, LANES), lambda bi, h: (bi, 0, 0)),
                  head(MLA_V_DIM)] + slab_in,
        out_specs=[head(MLA_V_DIM)] + slab_out,
        out_shape=[jax.ShapeDtypeStruct((b, s, MLA_WIDTH), F32)] + slab_shapes,
        compiler_params=pltpu.CompilerParams(dimension_semantics=("arbitrary",) * 2, vmem_limit_bytes=VMEM_LIMIT),
        name="mla_attn",
    )(q, kn, kr, v, *weights)


def _out_kernel(osb_ref, omla_ref, x_ref, g_sb_ref, g_mla_ref, w_o_ref, g_post_ref, g_ffn_ref, x1_ref, h2_ref):
    rows = osb_ref.shape[0]
    chunks = [slice(r, r + OUT_CHUNK) for r in range(0, rows, OUT_CHUNK)]

    def product(rs):
        n_sb = _rmsnorm(osb_ref[rs, :], g_sb_ref[...]).astype(BF16)
        n_mla = _rmsnorm(omla_ref[rs, :], g_mla_ref[...]).astype(BF16)
        return _dot(jnp.concatenate([n_sb, n_mla], axis=1), w_o_ref[...])

    def finish(rs, y):
        x1 = x_ref[rs, :] + _rmsnorm(y, g_post_ref[...])
        x1_ref[rs, :] = x1
        h2_ref[rs, :] = _rmsnorm(x1, g_ffn_ref[...]).astype(BF16)

    pending = None
    for rs in chunks:
        y = product(rs)
        if pending is not None:
            finish(*pending)
        pending = (rs, y)
    finish(*pending)


def _out_proj(o_sb, o_mla, x2, g_sb, g_mla, w_o, g_post, g_ffn):
    n = x2.shape[0]
    tm = OUT_ROWS
    row = lambda w: pl.BlockSpec((tm, w), lambda i: (i, 0))
    full = _resident
    return pl.pallas_call(
        _out_kernel,
        grid=(n // tm,),
        in_specs=[row(SB_WIDTH), row(MLA_WIDTH), row(D_MODEL), full(g_sb), full(g_mla), full(w_o), full(g_post),
                  full(g_ffn)],
        out_specs=[row(D_MODEL), row(D_MODEL)],
        out_shape=[jax.ShapeDtypeStruct((n, D_MODEL), F32), jax.ShapeDtypeStruct((n, D_MODEL), BF16)],
        compiler_params=pltpu.CompilerParams(dimension_semantics=("arbitrary",), vmem_limit_bytes=VMEM_LIMIT),
        name="out_proj",
    )(o_sb, o_mla, x2, g_sb, g_mla, w_o, g_post, g_ffn)


def _ffn_kernel(seq_tiles, h_ref, halo_ref, x1_ref, wg_ref, wv_ref, cwg_ref, cwv_ref, cbg_ref, cbv_ref, wd_ref,
                g_post_ref, o_ref, hcat_ref, pg_ref, pv_ref, act_ref, acc_ref):
    tm = FFN_ROWS
    i = pl.program_id(0)
    j = pl.program_id(1)
    last = pl.num_programs(1) - 1

    def conv(p_ref, w_ref, cw_ref, cb_ref):
        p_ref[...] = _dot(hcat_ref[...], w_ref[...])
        u = cb_ref[...] + cw_ref[CONV_WIDTH - 1:CONV_WIDTH, :] * p_ref[HALO:HALO + tm, :]
        for tap in range(1, CONV_WIDTH):
            k = CONV_WIDTH - 1 - tap
            u = u + cw_ref[k:k + 1, :] * p_ref[HALO - tap:HALO - tap + tm, :]
        return u

    def build(slot):
        gate = conv(pg_ref, wg_ref, cwg_ref, cbg_ref)
        val = conv(pv_ref, wv_ref, cwv_ref, cbv_ref)
        act_ref[slot] = (jax.nn.gelu(gate, approximate=True) * val).astype(BF16)

    def multiply(slot):
        return _dot(act_ref[slot], wd_ref[...])

    @pl.when(j == 0)
    def _():
        first = (i % seq_tiles) == 0
        hcat_ref[0:HALO, :] = jnp.where(first, jnp.zeros_like(halo_ref[...]), halo_ref[...])
        hcat_ref[HALO:, :] = h_ref[...]
        acc_ref[...] = jnp.zeros_like(acc_ref)
        build(0)

    @pl.when(jnp.logical_and(j > 0, j < last))
    def _():
        slot = j % 2
        acc_ref[...] += multiply(1 - slot)
        build(slot)

    @pl.when(j == last)
    def _():
        y = acc_ref[...] + multiply((last - 1) % 2)
        o_ref[...] = x1_ref[...] + _rmsnorm(y, g_post_ref[...])


def _ffn(h2, x1, w_up_blocks, conv_w, conv_b, w_down, g_post, seq_len):
    n = h2.shape[0]
    tm, tn = FFN_ROWS, FFN_COLS
    nj = D_FF // tn
    halo_blocks = tm // HALO
    up_t = lambda j: jnp.minimum(j, nj - 1)
    return pl.pallas_call(
        functools.partial(_ffn_kernel, seq_len // tm),
        grid=(n // tm, nj + 1),
        in_specs=[pl.BlockSpec((tm, D_MODEL), lambda i, j: (i, 0)),
                  pl.BlockSpec((HALO, D_MODEL), lambda i, j: (jnp.maximum(i * halo_blocks - 1, 0), 0)),
                  pl.BlockSpec((tm, D_MODEL), lambda i, j: (i, 0)),
                  pl.BlockSpec((None, D_MODEL, tn), lambda i, j: (up_t(j), 0, 0)),
                  pl.BlockSpec((None, D_MODEL, tn), lambda i, j: (up_t(j) + nj, 0, 0)),
                  pl.BlockSpec((CONV_WIDTH, tn), lambda i, j: (0, up_t(j))),
                  pl.BlockSpec((CONV_WIDTH, tn), lambda i, j: (0, up_t(j) + nj)),
                  pl.BlockSpec((1, tn), lambda i, j: (0, up_t(j))),
                  pl.BlockSpec((1, tn), lambda i, j: (0, up_t(j) + nj)),
                  pl.BlockSpec((tn, D_MODEL), lambda i, j: (jnp.maximum(j - 1, 0), 0)),
                  pl.BlockSpec((1, D_MODEL), lambda i, j: (0, 0))],
        out_specs=pl.BlockSpec((tm, D_MODEL), lambda i, j: (i, 0)),
        out_shape=jax.ShapeDtypeStruct((n, D_MODEL), F32),
        scratch_shapes=[pltpu.VMEM((tm + HALO, D_MODEL), BF16), pltpu.VMEM((tm + HALO, tn), F32),
                        pltpu.VMEM((tm + HALO, tn), F32), pltpu.VMEM((2, tm, tn), BF16),
                        pltpu.VMEM((tm, D_MODEL), F32)],
        compiler_params=pltpu.CompilerParams(dimension_semantics=("arbitrary", "arbitrary"),
                                             vmem_limit_bytes=VMEM_LIMIT),
        name="ffn",
    )(h2, h2, x1, w_up_blocks, w_up_blocks, conv_w, conv_w, conv_b, conv_b, w_down, g_post)


def _swap_halves(w):
    half = w.shape[-1] // 2
    return jnp.concatenate([w[..., half:], w[..., :half]], axis=-1)


def _layout_w_kr(w_in):
    k_rope = w_in[:, _C_KR:_C_KR + MLA_ROPE_DIM]
    return jnp.concatenate([k_rope, _swap_halves(k_rope)], axis=1).astype(BF16)


def _layout_w_uq(w_uq):
    w = w_uq.reshape(MLA_Q_RANK, MLA_HEADS, MLA_QK_DIM)
    rope = w[:, :, MLA_NOPE_DIM:]
    w = jnp.concatenate([w[:, :, :MLA_NOPE_DIM], rope, _swap_halves(rope)], axis=-1)
    return w.reshape(MLA_Q_RANK, MLA_HEADS * MLA_QK_PAD).astype(BF16)


def _layout_w_ukv(w_ukv):
    w = w_ukv.reshape(MLA_KV_RANK, MLA_HEADS, MLA_NOPE_DIM + MLA_V_DIM)
    k_nope = w[:, :, :MLA_NOPE_DIM].reshape(MLA_KV_RANK, MLA_WIDTH)
    v = w[:, :, MLA_NOPE_DIM:].reshape(MLA_KV_RANK, MLA_WIDTH)
    return jnp.concatenate([k_nope, v], axis=1).astype(BF16)


def kernel(x, positions, g_attn_pre, w_in, g_cq, w_uq, g_ckv, w_ukv, g_out_sb, g_out_mla, w_o, g_attn_post,
           g_ffn_pre, w_up, conv_w, conv_b, w_down, g_ffn_post):
    b, s, d = x.shape
    depth = w_in.shape[0]
    n = b * s
    half = MLA_ROPE_DIM // 2
    inv_freq = ROPE_THETA ** (-jnp.arange(half, dtype=F32) / half)
    invf = jnp.tile(inv_freq, LANES // half)[None, :]
    pos2 = positions.reshape(n, 1)
    x2 = x.reshape(n, d)
    r2 = lambda g: g[None, :]
    for l in range(depth):
        qsb, ksb, vsb, qm, kn, kr, vm = _proj(
            x2, pos2, invf, r2(g_attn_pre[l]), w_in[l].astype(BF16), _layout_w_kr(w_in[l]), r2(g_cq[l]),
            _layout_w_uq(w_uq[l]), r2(g_ckv[l]), _layout_w_ukv(w_ukv[l]))
        b3 = lambda a: a.reshape(b, s, a.shape[-1])
        o_sb, w_up_b = _sb_attention(b3(qsb), b3(ksb), b3(vsb), [w_up[l]], [FFN_COLS])
        o_mla, w_o_b, w_down_b = _mla_attention(b3(qm), b3(kn), b3(kr), b3(vm), [w_o[l], w_down[l]], [None, None])
        x1, h2 = _out_proj(o_sb.reshape(n, SB_WIDTH), o_mla.reshape(n, MLA_WIDTH), x2, r2(g_out_sb[l]),
                           r2(g_out_mla[l]), w_o_b, r2(g_attn_post[l]), r2(g_ffn_pre[l]))
        x2 = _ffn(h2, x1, w_up_b, conv_w[l], r2(conv_b[l]), w_down_b, r2(g_ffn_post[l]), s)
    return x2.reshape(b, s, d)
```

```python
import functools

import jax
import jax.numpy as jnp
from jax import lax
from jax.experimental import pallas as pl
from jax.experimental.pallas import tpu as pltpu

F32 = jnp.float32
BF16 = jnp.bfloat16

D_MODEL = 2048
CHUNK = 64
SB_HEADS = 8
SB_HEAD_DIM = 128
SB_WIDTH = SB_HEADS * SB_HEAD_DIM
MLA_HEADS = 8
MLA_NOPE_DIM = 128
MLA_ROPE_DIM = 64
MLA_V_DIM = 128
MLA_Q_RANK = 512
MLA_KV_RANK = 256
MLA_QK_DIM = MLA_NOPE_DIM + MLA_ROPE_DIM
MLA_WIDTH = MLA_HEADS * MLA_V_DIM
D_FF = 5632
CONV_WIDTH = 3
ROPE_THETA = 10000.0
EPS = 1e-6
LOG2_E = 1.4426950408889634

LANES = 128
BF16_SUBLANES = 16
MLA_QK_PAD = 2 * LANES

_C_QSB = 0
_C_KSB = SB_WIDTH
_C_VSB = 2 * SB_WIDTH
_C_CQ = 3 * SB_WIDTH
_C_CKV = _C_CQ + MLA_Q_RANK
_C_KR = _C_CKV + MLA_KV_RANK

PROJ_ROWS = 512
PROJ_CHUNK = 256
ATTN_TILE = 256
OUT_ROWS = 512
OUT_CHUNK = 256
FFN_ROWS = 512
FFN_COLS = 512
FFN_UP_COLS = 256
FFN_DOWN_COLS = 512
FFN_ACT_ROWS = 16
HALO = BF16_SUBLANES

VMEM_LIMIT = 56 * 1024 * 1024


def _rmsnorm(x, g):
    return x * lax.rsqrt(jnp.mean(x * x, axis=-1, keepdims=True) + EPS) * g


def _resident(a):
    return pl.BlockSpec(a.shape, lambda *_: (0,) * a.ndim, pipeline_mode=pl.Buffered(1))


def _dot(a, b):
    return jnp.dot(a, b, preferred_element_type=F32)


def _dot_nt(a, b):
    return lax.dot_general(a, b, (((1,), (1,)), ((), ())), preferred_element_type=F32)


def _proj_kernel(x_ref, pos_ref, invf_ref, g_pre_ref, w_in_ref, w_kr_ref, g_cq_ref, w_uq_ref, g_ckv_ref, w_ukv_ref,
                 qsb_ref, ksb_ref, vsb_ref, qm_ref, kn_ref, kr_ref, vm_ref):
    sb_scale = SB_HEAD_DIM ** -0.5 * LOG2_E
    mla_scale = MLA_QK_DIM ** -0.5 * LOG2_E
    def wide(rs):
        h = _rmsnorm(x_ref[rs, :], g_pre_ref[...]).astype(BF16)

        def proj(lo, hi):
            return _dot(h, w_in_ref[:, lo:hi])

        cq = proj(_C_CQ, _C_CKV)
        ckv = proj(_C_CKV, _C_KR)
        kr = _dot(h, w_kr_ref[...])
        qsb_ref[rs, :] = (proj(_C_QSB, _C_KSB) * sb_scale).astype(BF16)
        ksb_ref[rs, :] = proj(_C_KSB, _C_VSB).astype(BF16)
        vsb_ref[rs, :] = proj(_C_VSB, _C_CQ).astype(BF16)
        return cq, ckv, kr

    def latent(rs, cq, ckv, kr):
        ang = pos_ref[rs, :].astype(F32) * invf_ref[...]
        lane = lax.broadcasted_iota(jnp.int32, ang.shape, 1)
        cos = jnp.cos(ang)
        sin = jnp.sin(ang)
        t1 = jnp.where(lane < MLA_ROPE_DIM, cos, 0.0)
        t2 = jnp.where(lane < MLA_ROPE_DIM // 2, -sin, jnp.where(lane < MLA_ROPE_DIM, sin, 0.0))

        def rope(g):
            return g * t1 + pltpu.roll(g, MLA_ROPE_DIM, axis=1) * t2

        kr_ref[rs, :] = rope(kr).astype(BF16)

        cq = _rmsnorm(cq, g_cq_ref[...]).astype(BF16)
        for hd in range(MLA_HEADS):
            c0 = hd * MLA_QK_PAD
            qh = _dot(cq, w_uq_ref[:, c0:c0 + MLA_QK_PAD])
            qm_ref[rs, c0:c0 + LANES] = (qh[:, :LANES] * mla_scale).astype(BF16)
            qm_ref[rs, c0 + LANES:c0 + MLA_QK_PAD] = (rope(qh[:, LANES:]) * mla_scale).astype(BF16)

        ckv = _rmsnorm(ckv, g_ckv_ref[...]).astype(BF16)
        kn_ref[rs, :] = _dot(ckv, w_ukv_ref[:, :MLA_WIDTH]).astype(BF16)
        vm_ref[rs, :] = _dot(ckv, w_ukv_ref[:, MLA_WIDTH:]).astype(BF16)

    pending = None
    for r in range(0, x_ref.shape[0], PROJ_CHUNK):
        rs = slice(r, r + PROJ_CHUNK)
        latents = wide(rs)
        if pending is not None:
            latent(*pending)
        pending = (rs,) + latents
    latent(*pending)


def _proj(x2, pos2, invf, g_pre, w_in_b, w_kr_p, g_cq, w_uq_p, g_ckv, w_ukv_p):
    n = x2.shape[0]
    tm = PROJ_ROWS
    row = lambda w: pl.BlockSpec((tm, w), lambda i: (i, 0))
    full = _resident
    out_widths = (SB_WIDTH, SB_WIDTH, SB_WIDTH, MLA_HEADS * MLA_QK_PAD, MLA_WIDTH, LANES, MLA_WIDTH)
    return pl.pallas_call(
        _proj_kernel,
        grid=(n // tm,),
        in_specs=[row(D_MODEL), row(1), full(invf), full(g_pre), full(w_in_b), full(w_kr_p), full(g_cq),
                  full(w_uq_p), full(g_ckv), full(w_ukv_p)],
        out_specs=[row(w) for w in out_widths],
        out_shape=[jax.ShapeDtypeStruct((n, w), BF16) for w in out_widths],
        compiler_params=pltpu.CompilerParams(dimension_semantics=("arbitrary",), vmem_limit_bytes=VMEM_LIMIT),
        name="proj",
    )(x2, pos2, invf, g_pre, w_in_b, w_kr_p, g_cq, w_uq_p, g_ckv, w_ukv_p)


def _softplus2(z):
    return jnp.maximum(z, 0.0) + jnp.log2(1.0 + jnp.exp2(-jnp.abs(z)))


def _cast_slabs(src_refs, dst_refs):
    for src_ref, dst_ref in zip(src_refs, dst_refs):
        if len(dst_ref.shape) == 3:
            width = dst_ref.shape[2]
            for c in range(dst_ref.shape[0]):
                dst_ref[c] = src_ref[:, c * width:(c + 1) * width].astype(BF16)
        else:
            dst_ref[...] = src_ref[...].astype(BF16)


def _slab_specs(weights, col_blocks, batch, heads):
    in_specs, out_specs, out_shapes = [], [], []
    for w, width in zip(weights, col_blocks):
        rows = w.shape[0] // (batch * heads)
        in_specs.append(pl.BlockSpec((rows, w.shape[1]), lambda bi, h: (bi * heads + h, 0)))
        if width is None:
            out_specs.append(pl.BlockSpec((rows, w.shape[1]), lambda bi, h: (bi * heads + h, 0)))
            out_shapes.append(jax.ShapeDtypeStruct(w.shape, BF16))
        else:
            n_blocks = w.shape[1] // width
            out_specs.append(pl.BlockSpec((n_blocks, rows, width), lambda bi, h: (0, bi * heads + h, 0)))
            out_shapes.append(jax.ShapeDtypeStruct((n_blocks, w.shape[0], width), BF16))
    return in_specs, out_specs, out_shapes


def _sb_kernel(n_cast, q_ref, k_ref, v_ref, *refs):
    o_ref = refs[n_cast]
    _cast_slabs(refs[:n_cast], refs[n_cast + 1:])
    t = ATTN_TILE
    row = lax.broadcasted_iota(jnp.int32, (t, t), 0)
    col = lax.broadcasted_iota(jnp.int32, (t, t), 1)
    mask = col < row
    tri = (row >= col).astype(BF16)
    tri2 = jnp.concatenate([tri, tri], axis=0)

    n_tiles = q_ref.shape[0] // t

    def logits(qi):
        return _dot_nt(q_ref[qi * t:(qi + 1) * t, :], k_ref[0:(qi + 1) * t, :])

    def suffix_sums(zj, diag):
        sp = _softplus2(zj)
        if diag:
            sp = jnp.where(mask, sp, 0.0)
        hi = sp.astype(BF16)
        lo = (sp - hi.astype(F32)).astype(BF16)
        return _dot(jnp.concatenate([hi, lo], axis=1), tri2)

    z = [logits(0)] + [None] * (n_tiles - 1)
    a_tiles = [[None] * (qi + 1) for qi in range(n_tiles)]
    carry = [None] * n_tiles

    def finish(qi, j, zj, local):
        cum = local if carry[qi] is None else local + carry[qi]
        a = jnp.exp2(zj - cum)
        if j == qi:
            a = jnp.where(mask, a, 0.0)
        a_tiles[qi][j] = a.astype(BF16)
        carry[qi] = local[:, 0:1] if carry[qi] is None else carry[qi] + local[:, 0:1]
        if j == 0:
            a_all = a_tiles[qi][0] if qi == 0 else jnp.concatenate(a_tiles[qi], axis=1)
            o_ref[qi * t:(qi + 1) * t, :] = _dot(a_all, v_ref[0:(qi + 1) * t, :])

    pending = None
    for qi in range(n_tiles):
        for j in range(qi, -1, -1):
            if j == qi and qi + 1 < n_tiles:
                z[qi + 1] = logits(qi + 1)
            zj = z[qi][:, j * t:(j + 1) * t]
            local = suffix_sums(zj, j == qi)
            if pending is not None:
                finish(*pending)
            pending = (qi, j, zj, local)
    finish(*pending)


def _sb_attention(q, k, v, weights, col_blocks):
    b, s, _ = q.shape
    head = pl.BlockSpec((None, s, SB_HEAD_DIM), lambda bi, h: (bi, 0, h))
    slab_in, slab_out, slab_shapes = _slab_specs(weights, col_blocks, b, SB_HEADS)
    return pl.pallas_call(
        functools.partial(_sb_kernel, len(weights)),
        grid=(b, SB_HEADS),
        in_specs=[head, head, head] + slab_in,
        out_specs=[head] + slab_out,
        out_shape=[jax.ShapeDtypeStruct((b, s, SB_WIDTH), F32)] + slab_shapes,
        compiler_params=pltpu.CompilerParams(dimension_semantics=("arbitrary",) * 2, vmem_limit_bytes=VMEM_LIMIT),
        name="sb_attn",
    )(q, k, v, *weights)


def _mla_kernel(n_cast, q_ref, kn_ref, kr_ref, v_ref, *refs):
    o_ref = refs[n_cast]
    _cast_slabs(refs[:n_cast], refs[n_cast + 1:])
    t = ATTN_TILE
    row = lax.broadcasted_iota(jnp.int32, (t, t), 0)
    col = lax.broadcasted_iota(jnp.int32, (t, t), 1)
    mask = (col // CHUNK) <= (row // CHUNK)

    def scores(qi):
        n = qi + 1
        kcat = jnp.concatenate([kn_ref[0:n * t, :], kr_ref[0:n * t, :]], axis=1)
        return _dot_nt(q_ref[qi * t:n * t, :], kcat)

    n_tiles = q_ref.shape[0] // t
    sc_next = scores(0)
    for qi in range(n_tiles):
        n = qi + 1
        sc = sc_next
        if n < n_tiles:
            sc_next = scores(n)
        diag = jnp.where(mask, sc[:, qi * t:], -jnp.inf)
        sc = diag if n == 1 else jnp.concatenate([sc[:, :qi * t], diag], axis=1)
        p = jnp.exp2(sc - jnp.max(sc, axis=-1, keepdims=True))
        vcat = jnp.concatenate([v_ref[0:n * t, :], jnp.ones((n * t, LANES), BF16)], axis=1)
        acc = _dot(p.astype(BF16), vcat)
        o_ref[qi * t:n * t, :] = acc[:, :MLA_V_DIM] / acc[:, MLA_V_DIM:]


def _mla_attention(q, kn, kr, v, weights, col_blocks):
    b, s, _ = kn.shape
    head = lambda w: pl.BlockSpec((None, s, w), lambda bi, h: (bi, 0, h))
    slab_in, slab_out, slab_shapes = _slab_specs(weights, col_blocks, b, MLA_HEADS)
    return pl.pallas_call(
        functools.partial(_mla_kernel, len(weights)),
        grid=(b, MLA_HEADS),
        in_specs=[head(MLA_QK_PAD), head(MLA_NOPE_DIM), pl.BlockSpec((None, s, LANES), lambda bi, h: (bi, 0, 0)),
                  head(MLA_V_DIM)] + slab_in,
        out_specs=[head(MLA_V_DIM)] + slab_out,
        out_shape=[jax.ShapeDtypeStruct((b, s, MLA_WIDTH), F32)] + slab_shapes,
        compiler_params=pltpu.CompilerParams(dimension_semantics=("arbitrary",) * 2, vmem_limit_bytes=VMEM_LIMIT),
        name="mla_attn",
    )(q, kn, kr, v, *weights)


def _out_kernel(osb_ref, omla_ref, x_ref, g_sb_ref, g_mla_ref, w_o_ref, g_post_ref, g_ffn_ref, x1_ref, h2_ref):
    rows = osb_ref.shape[0]
    chunks = [slice(r, r + OUT_CHUNK) for r in range(0, rows, OUT_CHUNK)]

    def product(rs):
        n_sb = _rmsnorm(osb_ref[rs, :], g_sb_ref[...]).astype(BF16)
        n_mla = _rmsnorm(omla_ref[rs, :], g_mla_ref[...]).astype(BF16)
        return _dot(jnp.concatenate([n_sb, n_mla], axis=1), w_o_ref[...])

    def finish(rs, y):
        x1 = x_ref[rs, :] + _rmsnorm(y, g_post_ref[...])
        x1_ref[rs, :] = x1
        h2_ref[rs, :] = _rmsnorm(x1, g_ffn_ref[...]).astype(BF16)

    pending = None
    for rs in chunks:
        y = product(rs)
        if pending is not None:
            finish(*pending)
        pending = (rs, y)
    finish(*pending)


def _out_proj(o_sb, o_mla, x2, g_sb, g_mla, w_o, g_post, g_ffn):
    n = x2.shape[0]
    tm = OUT_ROWS
    row = lambda w: pl.BlockSpec((tm, w), lambda i: (i, 0))
    full = _resident
    return pl.pallas_call(
        _out_kernel,
        grid=(n // tm,),
        in_specs=[row(SB_WIDTH), row(MLA_WIDTH), row(D_MODEL), full(g_sb), full(g_mla), full(w_o), full(g_post),
                  full(g_ffn)],
        out_specs=[row(D_MODEL), row(D_MODEL)],
        out_shape=[jax.ShapeDtypeStruct((n, D_MODEL), F32), jax.ShapeDtypeStruct((n, D_MODEL), BF16)],
        compiler_params=pltpu.CompilerParams(dimension_semantics=("arbitrary",), vmem_limit_bytes=VMEM_LIMIT),
        name="out_proj",
    )(o_sb, o_mla, x2, g_sb, g_mla, w_o, g_post, g_ffn)


def _interleave(sparse, dense):
    sparse, dense = list(sparse), list(dense)
    per = -(-len(dense) // max(len(sparse), 1))
    while sparse or dense:
        for _ in range(per if sparse else len(dense)):
            if dense:
                dense.pop(0)()
        if sparse:
            sparse.pop(0)()


def _ffn_kernel(seq_tiles, nj, h_ref, halo_ref, x1_ref, wg_ref, wv_ref, cwg_ref, cwv_ref, cbg_ref, cbv_ref, wd_ref,
                g_post_ref, o_ref, hcat_ref, pg0_ref, pg1_ref, pv0_ref, pv1_ref, act0_ref, act1_ref, acc_ref):
    tm = FFN_ROWS
    i = pl.program_id(0)
    j = pl.program_id(1)
    pg_refs, pv_refs, act_refs = (pg0_ref, pg1_ref), (pv0_ref, pv1_ref), (act0_ref, act1_ref)

    def project(slot):
        def piece(p_ref, w_ref, cs):
            def run():
                p_ref[:, cs] = _dot(hcat_ref[...], w_ref[:, cs])
            return run
        return [piece(p_ref, w_ref, slice(c, c + FFN_UP_COLS)) for c in range(0, FFN_COLS, FFN_UP_COLS)
                for p_ref, w_ref in ((pg_refs[slot], wg_ref), (pv_refs[slot], wv_ref))]

    def activate(slot):
        def conv(p_ref, cw_ref, cb_ref, r):
            u = cb_ref[...] + cw_ref[CONV_WIDTH - 1:CONV_WIDTH, :] * p_ref[HALO + r:HALO + r + FFN_ACT_ROWS, :]
            for tap in range(1, CONV_WIDTH):
                k = CONV_WIDTH - 1 - tap
                u = u + cw_ref[k:k + 1, :] * p_ref[HALO - tap + r:HALO - tap + r + FFN_ACT_ROWS, :]
            return u

        def piece(r):
            def run():
                gate = conv(pg_refs[slot], cwg_ref, cbg_ref, r)
                val = conv(pv_refs[slot], cwv_ref, cbv_ref, r)
                act_refs[slot][r:r + FFN_ACT_ROWS, :] = (jax.nn.gelu(gate, approximate=True) * val).astype(BF16)
            return run
        return [piece(r) for r in range(0, tm, FFN_ACT_ROWS)]

    def multiply(slot):
        def piece(cs):
            def run():
                acc_ref[:, cs] += _dot(act_refs[slot][...], wd_ref[:, cs])
            return run
        return [piece(slice(c, c + FFN_DOWN_COLS)) for c in range(0, D_MODEL, FFN_DOWN_COLS)]

    @pl.when(j == 0)
    def _():
        first = (i % seq_tiles) == 0
        hcat_ref[0:HALO, :] = jnp.where(first, jnp.zeros_like(halo_ref[...]), halo_ref[...])
        hcat_ref[HALO:, :] = h_ref[...]
        acc_ref[...] = jnp.zeros_like(acc_ref)
        _interleave(project(0), [])

    @pl.when(j == 1)
    def _():
        _interleave(project(1), activate(0))

    for parity in range(2):
        @pl.when(jnp.logical_and(jnp.logical_and(j >= 2, j < nj), j % 2 == parity))
        def _():
            _interleave(project(parity) + multiply(parity), activate(1 - parity))

    @pl.when(j == nj)
    def _():
        _interleave(multiply(nj % 2), activate((nj - 1) % 2))

    @pl.when(j == nj + 1)
    def _():
        _interleave(multiply((nj - 1) % 2), [])
        o_ref[...] = x1_ref[...] + _rmsnorm(acc_ref[...], g_post_ref[...])


def _ffn(h2, x1, w_up_blocks, conv_w, conv_b, w_down, g_post, seq_len):
    n = h2.shape[0]
    tm, tn = FFN_ROWS, FFN_COLS
    nj = D_FF // tn
    halo_blocks = tm // HALO
    tile = lambda stage_lag: (lambda j: jnp.clip(j - stage_lag, 0, nj - 1))
    proj_t, act_t, mul_t = tile(0), tile(1), tile(2)
    return pl.pallas_call(
        functools.partial(_ffn_kernel, seq_len // tm, nj),
        grid=(n // tm, nj + 2),
        in_specs=[pl.BlockSpec((tm, D_MODEL), lambda i, j: (i, 0)),
                  pl.BlockSpec((HALO, D_MODEL), lambda i, j: (jnp.maximum(i * halo_blocks - 1, 0), 0)),
                  pl.BlockSpec((tm, D_MODEL), lambda i, j: (i, 0)),
                  pl.BlockSpec((None, D_MODEL, tn), lambda i, j: (proj_t(j), 0, 0)),
                  pl.BlockSpec((None, D_MODEL, tn), lambda i, j: (proj_t(j) + nj, 0, 0)),
                  pl.BlockSpec((CONV_WIDTH, tn), lambda i, j: (0, act_t(j))),
                  pl.BlockSpec((CONV_WIDTH, tn), lambda i, j: (0, act_t(j) + nj)),
                  pl.BlockSpec((1, tn), lambda i, j: (0, act_t(j))),
                  pl.BlockSpec((1, tn), lambda i, j: (0, act_t(j) + nj)),
                  pl.BlockSpec((tn, D_MODEL), lambda i, j: (mul_t(j), 0)),
                  pl.BlockSpec((1, D_MODEL), lambda i, j: (0, 0))],
        out_specs=pl.BlockSpec((tm, D_MODEL), lambda i, j: (i, 0)),
        out_shape=jax.ShapeDtypeStruct((n, D_MODEL), F32),
        scratch_shapes=[pltpu.VMEM((tm + HALO, D_MODEL), BF16)] + [pltpu.VMEM((tm + HALO, tn), F32)] * 4 +
                       [pltpu.VMEM((tm, tn), BF16)] * 2 + [pltpu.VMEM((tm, D_MODEL), F32)],
        compiler_params=pltpu.CompilerParams(dimension_semantics=("arbitrary", "arbitrary"),
                                             vmem_limit_bytes=VMEM_LIMIT),
        name="ffn",
    )(h2, h2, x1, w_up_blocks, w_up_blocks, conv_w, conv_w, conv_b, conv_b, w_down, g_post)


def _swap_halves(w):
    half = w.shape[-1] // 2
    return jnp.concatenate([w[..., half:], w[..., :half]], axis=-1)


def _layout_w_kr(w_in):
    k_rope = w_in[:, _C_KR:_C_KR + MLA_ROPE_DIM]
    return jnp.concatenate([k_rope, _swap_halves(k_rope)], axis=1).astype(BF16)


def _layout_w_uq(w_uq):
    w = w_uq.reshape(MLA_Q_RANK, MLA_HEADS, MLA_QK_DIM)
    rope = w[:, :, MLA_NOPE_DIM:]
    w = jnp.concatenate([w[:, :, :MLA_NOPE_DIM], rope, _swap_halves(rope)], axis=-1)
    return w.reshape(MLA_Q_RANK, MLA_HEADS * MLA_QK_PAD).astype(BF16)


def _layout_w_ukv(w_ukv):
    w = w_ukv.reshape(MLA_KV_RANK, MLA_HEADS, MLA_NOPE_DIM + MLA_V_DIM)
    k_nope = w[:, :, :MLA_NOPE_DIM].reshape(MLA_KV_RANK, MLA_WIDTH)
    v = w[:, :, MLA_NOPE_DIM:].reshape(MLA_KV_RANK, MLA_WIDTH)
    return jnp.concatenate([k_nope, v], axis=1).astype(BF16)


def kernel(x, positions, g_attn_pre, w_in, g_cq, w_uq, g_ckv, w_ukv, g_out_sb, g_out_mla, w_o, g_attn_post,
           g_ffn_pre, w_up, conv_w, conv_b, w_down, g_ffn_post):
    b, s, d = x.shape
    depth = w_in.shape[0]
    n = b * s
    half = MLA_ROPE_DIM // 2
    inv_freq = ROPE_THETA ** (-jnp.arange(half, dtype=F32) / half)
    invf = jnp.tile(inv_freq, LANES // half)[None, :]
    pos2 = positions.reshape(n, 1)
    x2 = x.reshape(n, d)
    r2 = lambda g: g[None, :]
    for l in range(depth):
        qsb, ksb, vsb, qm, kn, kr, vm = _proj(
            x2, pos2, invf, r2(g_attn_pre[l]), w_in[l].astype(BF16), _layout_w_kr(w_in[l]), r2(g_cq[l]),
            _layout_w_uq(w_uq[l]), r2(g_ckv[l]), _layout_w_ukv(w_ukv[l]))
        b3 = lambda a: a.reshape(b, s, a.shape[-1])
        o_sb, w_up_b = _sb_attention(b3(qsb), b3(ksb), b3(vsb), [w_up[l]], [FFN_COLS])
        o_mla, w_o_b, w_down_b = _mla_attention(b3(qm), b3(kn), b3(kr), b3(vm), [w_o[l], w_down[l]], [None, None])
        x1, h2 = _out_proj(o_sb.reshape(n, SB_WIDTH), o_mla.reshape(n, MLA_WIDTH), x2, r2(g_out_sb[l]),
                           r2(g_out_mla[l]), w_o_b, r2(g_attn_post[l]), r2(g_ffn_pre[l]))
        x2 = _ffn(h2, x1, w_up_b, conv_w[l], r2(conv_b[l]), w_down_b, r2(g_ffn_post[l]), s)
    return x2.reshape(b, s, d)
```

```python
import functools

import jax
import jax.numpy as jnp
from jax import lax
from jax.experimental import pallas as pl
from jax.experimental.pallas import tpu as pltpu

F32 = jnp.float32
BF16 = jnp.bfloat16

D_MODEL = 2048
CHUNK = 64
SB_HEADS = 8
SB_HEAD_DIM = 128
SB_WIDTH = SB_HEADS * SB_HEAD_DIM
MLA_HEADS = 8
MLA_NOPE_DIM = 128
MLA_ROPE_DIM = 64
MLA_V_DIM = 128
MLA_Q_RANK = 512
MLA_KV_RANK = 256
MLA_QK_DIM = MLA_NOPE_DIM + MLA_ROPE_DIM
MLA_WIDTH = MLA_HEADS * MLA_V_DIM
D_FF = 5632
CONV_WIDTH = 3
ROPE_THETA = 10000.0
EPS = 1e-6
LOG2_E = 1.4426950408889634

LANES = 128
BF16_SUBLANES = 16
MLA_QK_PAD = 2 * LANES

_C_QSB = 0
_C_KSB = SB_WIDTH
_C_VSB = 2 * SB_WIDTH
_C_CQ = 3 * SB_WIDTH
_C_CKV = _C_CQ + MLA_Q_RANK
_C_KR = _C_CKV + MLA_KV_RANK

PROJ_ROWS = 512
PROJ_CHUNK = 256
ATTN_TILE = 256
OUT_ROWS = 512
OUT_CHUNK = 256
FFN_ROWS = 512
FFN_COLS = 512
HALO = BF16_SUBLANES

VMEM_LIMIT = 56 * 1024 * 1024


def _rmsnorm(x, g):
    return x * lax.rsqrt(jnp.mean(x * x, axis=-1, keepdims=True) + EPS) * g


def _resident(a):
    return pl.BlockSpec(a.shape, lambda *_: (0,) * a.ndim, pipeline_mode=pl.Buffered(1))


def _dot(a, b):
    return jnp.dot(a, b, preferred_element_type=F32)


def _dot_nt(a, b):
    return lax.dot_general(a, b, (((1,), (1,)), ((), ())), preferred_element_type=F32)


def _proj_kernel(x_ref, pos_ref, invf_ref, g_pre_ref, w_in_ref, w_kr_ref, g_cq_ref, w_uq_ref, g_ckv_ref, w_ukv_ref,
                 qsb_ref, ksb_ref, vsb_ref, qm_ref, kn_ref, kr_ref, vm_ref):
    sb_scale = SB_HEAD_DIM ** -0.5 * LOG2_E
    mla_scale = MLA_QK_DIM ** -0.5 * LOG2_E

    def wide(rs):
        h = _rmsnorm(x_ref[rs, :], g_pre_ref[...]).astype(BF16)

        def proj(lo, hi):
            return _dot(h, w_in_ref[:, lo:hi])

        cq = proj(_C_CQ, _C_CKV)
        ckv = proj(_C_CKV, _C_KR)
        kr = _dot(h, w_kr_ref[...])
        qsb_ref[rs, :] = (proj(_C_QSB, _C_KSB) * sb_scale).astype(BF16)
        ksb_ref[rs, :] = proj(_C_KSB, _C_VSB).astype(BF16)
        vsb_ref[rs, :] = proj(_C_VSB, _C_CQ).astype(BF16)
        return cq, ckv, kr

    def latent(rs, cq, ckv, kr):
        ang = pos_ref[rs, :].astype(F32) * invf_ref[...]
        lane = lax.broadcasted_iota(jnp.int32, ang.shape, 1)
        cos = jnp.cos(ang)
        sin = jnp.sin(ang)
        t1 = jnp.where(lane < MLA_ROPE_DIM, cos, 0.0)
        t2 = jnp.where(lane < MLA_ROPE_DIM // 2, -sin, jnp.where(lane < MLA_ROPE_DIM, sin, 0.0))

        def rope(g):
            return g * t1 + pltpu.roll(g, MLA_ROPE_DIM, axis=1) * t2

        kr_ref[rs, :] = rope(kr).astype(BF16)

        cq = _rmsnorm(cq, g_cq_ref[...]).astype(BF16)
        for hd in range(MLA_HEADS):
            c0 = hd * MLA_QK_PAD
            qh = _dot(cq, w_uq_ref[:, c0:c0 + MLA_QK_PAD])
            qm_ref[rs, c0:c0 + LANES] = (qh[:, :LANES] * mla_scale).astype(BF16)
            qm_ref[rs, c0 + LANES:c0 + MLA_QK_PAD] = (rope(qh[:, LANES:]) * mla_scale).astype(BF16)

        ckv = _rmsnorm(ckv, g_ckv_ref[...]).astype(BF16)
        kn_ref[rs, :] = _dot(ckv, w_ukv_ref[:, :MLA_WIDTH]).astype(BF16)
        vm_ref[rs, :] = _dot(ckv, w_ukv_ref[:, MLA_WIDTH:]).astype(BF16)

    pending = None
    for r in range(0, x_ref.shape[0], PROJ_CHUNK):
        rs = slice(r, r + PROJ_CHUNK)
        latents = wide(rs)
        if pending is not None:
            latent(*pending)
        pending = (rs,) + latents
    latent(*pending)


def _proj(x2, pos2, invf, g_pre, w_in_b, w_kr_p, g_cq, w_uq_p, g_ckv, w_ukv_p):
    n = x2.shape[0]
    tm = PROJ_ROWS
    row = lambda w: pl.BlockSpec((tm, w), lambda i: (i, 0))
    full = _resident
    out_widths = (SB_WIDTH, SB_WIDTH, SB_WIDTH, MLA_HEADS * MLA_QK_PAD, MLA_WIDTH, LANES, MLA_WIDTH)
    return pl.pallas_call(
        _proj_kernel,
        grid=(n // tm,),
        in_specs=[row(D_MODEL), row(1), full(invf), full(g_pre), full(w_in_b), full(w_kr_p), full(g_cq),
                  full(w_uq_p), full(g_ckv), full(w_ukv_p)],
        out_specs=[row(w) for w in out_widths],
        out_shape=[jax.ShapeDtypeStruct((n, w), BF16) for w in out_widths],
        compiler_params=pltpu.CompilerParams(dimension_semantics=("arbitrary",), vmem_limit_bytes=VMEM_LIMIT),
        name="proj",
    )(x2, pos2, invf, g_pre, w_in_b, w_kr_p, g_cq, w_uq_p, g_ckv, w_ukv_p)


def _softplus2(z):
    return jnp.maximum(z, 0.0) + jnp.log2(1.0 + jnp.exp2(-jnp.abs(z)))


def _cast_slabs(src_refs, dst_refs):
    for src_ref, dst_ref in zip(src_refs, dst_refs):
        if len(dst_ref.shape) == 3:
            width = dst_ref.shape[2]
            for c in range(dst_ref.shape[0]):
                dst_ref[c] = src_ref[:, c * width:(c + 1) * width].astype(BF16)
        else:
            dst_ref[...] = src_ref[...].astype(BF16)


def _slab_specs(weights, col_blocks, batch, heads):
    in_specs, out_specs, out_shapes = [], [], []
    for w, width in zip(weights, col_blocks):
        rows = w.shape[0] // (batch * heads)
        in_specs.append(pl.BlockSpec((rows, w.shape[1]), lambda bi, h: (bi * heads + h, 0)))
        if width is None:
            out_specs.append(pl.BlockSpec((rows, w.shape[1]), lambda bi, h: (bi * heads + h, 0)))
            out_shapes.append(jax.ShapeDtypeStruct(w.shape, BF16))
        else:
            n_blocks = w.shape[1] // width
            out_specs.append(pl.BlockSpec((n_blocks, rows, width), lambda bi, h: (0, bi * heads + h, 0)))
            out_shapes.append(jax.ShapeDtypeStruct((n_blocks, w.shape[0], width), BF16))
    return in_specs, out_specs, out_shapes


def _sb_pieces(q_ref, k_ref, v_ref, o_ref):
    t = ATTN_TILE
    row = lax.broadcasted_iota(jnp.int32, (t, t), 0)
    col = lax.broadcasted_iota(jnp.int32, (t, t), 1)
    mask = col < row
    tri = (row >= col).astype(BF16)
    tri2 = jnp.concatenate([tri, tri], axis=0)

    n_tiles = q_ref.shape[0] // t

    def logits(qi):
        return _dot_nt(q_ref[qi * t:(qi + 1) * t, :], k_ref[0:(qi + 1) * t, :])

    def suffix_sums(zj, diag):
        sp = _softplus2(zj)
        if diag:
            sp = jnp.where(mask, sp, 0.0)
        hi = sp.astype(BF16)
        lo = (sp - hi.astype(F32)).astype(BF16)
        return _dot(jnp.concatenate([hi, lo], axis=1), tri2)

    z = [logits(0)] + [None] * (n_tiles - 1)
    a_tiles = [[None] * (qi + 1) for qi in range(n_tiles)]
    carry = [None] * n_tiles

    def finish(qi, j, zj, local):
        cum = local if carry[qi] is None else local + carry[qi]
        a = jnp.exp2(zj - cum)
        if j == qi:
            a = jnp.where(mask, a, 0.0)
        a_tiles[qi][j] = a.astype(BF16)
        carry[qi] = local[:, 0:1] if carry[qi] is None else carry[qi] + local[:, 0:1]
        if j == 0:
            a_all = a_tiles[qi][0] if qi == 0 else jnp.concatenate(a_tiles[qi], axis=1)
            o_ref[qi * t:(qi + 1) * t, :] = _dot(a_all, v_ref[0:(qi + 1) * t, :])

    pending = None
    for qi in range(n_tiles):
        for j in range(qi, -1, -1):
            if j == qi and qi + 1 < n_tiles:
                z[qi + 1] = logits(qi + 1)
            zj = z[qi][:, j * t:(j + 1) * t]
            local = suffix_sums(zj, j == qi)
            if pending is not None:
                finish(*pending)
            pending = (qi, j, zj, local)
            yield
    finish(*pending)
    yield


def _mla_pieces(q_ref, kn_ref, kr_ref, v_ref, o_ref):
    t = ATTN_TILE
    row = lax.broadcasted_iota(jnp.int32, (t, t), 0)
    col = lax.broadcasted_iota(jnp.int32, (t, t), 1)
    mask = (col // CHUNK) <= (row // CHUNK)

    def scores(qi):
        n = qi + 1
        kcat = jnp.concatenate([kn_ref[0:n * t, :], kr_ref[0:n * t, :]], axis=1)
        return _dot_nt(q_ref[qi * t:n * t, :], kcat)

    n_tiles = q_ref.shape[0] // t
    sc_next = scores(0)
    for qi in range(n_tiles):
        n = qi + 1
        sc = sc_next
        if n < n_tiles:
            sc_next = scores(n)
        yield
        diag = jnp.where(mask, sc[:, qi * t:], -jnp.inf)
        sc = diag if n == 1 else jnp.concatenate([sc[:, :qi * t], diag], axis=1)
        p = jnp.exp2(sc - jnp.max(sc, axis=-1, keepdims=True))
        yield
        vcat = jnp.concatenate([v_ref[0:n * t, :], jnp.ones((n * t, LANES), BF16)], axis=1)
        acc = _dot(p.astype(BF16), vcat)
        o_ref[qi * t:n * t, :] = acc[:, :MLA_V_DIM] / acc[:, MLA_V_DIM:]
        yield


def _attn_kernel(n_cast, qs_ref, ks_ref, vs_ref, qm_ref, kn_ref, kr_ref, vm_ref, *refs):
    osb_ref, omla_ref = refs[n_cast], refs[n_cast + 1]
    _cast_slabs(refs[:n_cast], refs[n_cast + 2:])
    n_tiles = qs_ref.shape[0] // ATTN_TILE
    sb = _sb_pieces(qs_ref, ks_ref, vs_ref, osb_ref)
    mla = _mla_pieces(qm_ref, kn_ref, kr_ref, vm_ref, omla_ref)
    mla_per_tile = 3
    for qi in range(n_tiles):
        order = sorted([(k / (qi + 1), 0) for k in range(qi + 1)] +
                       [((k + 0.5) / mla_per_tile, 1) for k in range(mla_per_tile)])
        for _, stream in order:
            next(mla if stream else sb)
    for _ in sb:
        pass
    for _ in mla:
        pass


def _attention(qs, ks, vs, qm, kn, kr, vm, weights, col_blocks):
    b, s, _ = qs.shape
    head = lambda w: pl.BlockSpec((None, s, w), lambda bi, h: (bi, 0, h))
    slab_in, slab_out, slab_shapes = _slab_specs(weights, col_blocks, b, SB_HEADS)
    return pl.pallas_call(
        functools.partial(_attn_kernel, len(weights)),
        grid=(b, SB_HEADS),
        in_specs=[head(SB_HEAD_DIM)] * 3 +
                 [head(MLA_QK_PAD), head(MLA_NOPE_DIM), pl.BlockSpec((None, s, LANES), lambda bi, h: (bi, 0, 0)),
                  head(MLA_V_DIM)] + slab_in,
        out_specs=[head(SB_HEAD_DIM), head(MLA_V_DIM)] + slab_out,
        out_shape=[jax.ShapeDtypeStruct((b, s, SB_WIDTH), F32), jax.ShapeDtypeStruct((b, s, MLA_WIDTH), F32)] +
                  slab_shapes,
        compiler_params=pltpu.CompilerParams(dimension_semantics=("arbitrary",) * 2, vmem_limit_bytes=VMEM_LIMIT),
        name="attn",
    )(qs, ks, vs, qm, kn, kr, vm, *weights)


def _out_kernel(osb_ref, omla_ref, x_ref, g_sb_ref, g_mla_ref, w_o_ref, g_post_ref, g_ffn_ref, x1_ref, h2_ref):
    rows = osb_ref.shape[0]
    chunks = [slice(r, r + OUT_CHUNK) for r in range(0, rows, OUT_CHUNK)]

    def product(rs):
        n_sb = _rmsnorm(osb_ref[rs, :], g_sb_ref[...]).astype(BF16)
        n_mla = _rmsnorm(omla_ref[rs, :], g_mla_ref[...]).astype(BF16)
        return _dot(jnp.concatenate([n_sb, n_mla], axis=1), w_o_ref[...])

    def finish(rs, y):
        x1 = x_ref[rs, :] + _rmsnorm(y, g_post_ref[...])
        x1_ref[rs, :] = x1
        h2_ref[rs, :] = _rmsnorm(x1, g_ffn_ref[...]).astype(BF16)

    pending = None
    for rs in chunks:
        y = product(rs)
        if pending is not None:
            finish(*pending)
        pending = (rs, y)
    finish(*pending)


def _out_proj(o_sb, o_mla, x2, g_sb, g_mla, w_o, g_post, g_ffn):
    n = x2.shape[0]
    tm = OUT_ROWS
    row = lambda w: pl.BlockSpec((tm, w), lambda i: (i, 0))
    full = _resident
    return pl.pallas_call(
        _out_kernel,
        grid=(n // tm,),
        in_specs=[row(SB_WIDTH), row(MLA_WIDTH), row(D_MODEL), full(g_sb), full(g_mla), full(w_o), full(g_post),
                  full(g_ffn)],
        out_specs=[row(D_MODEL), row(D_MODEL)],
        out_shape=[jax.ShapeDtypeStruct((n, D_MODEL), F32), jax.ShapeDtypeStruct((n, D_MODEL), BF16)],
        compiler_params=pltpu.CompilerParams(dimension_semantics=("arbitrary",), vmem_limit_bytes=VMEM_LIMIT),
        name="out_proj",
    )(o_sb, o_mla, x2, g_sb, g_mla, w_o, g_post, g_ffn)


def _ffn_kernel(seq_tiles, h_ref, halo_ref, x1_ref, wg_ref, wv_ref, cwg_ref, cwv_ref, cbg_ref, cbv_ref, wd_ref,
                g_post_ref, o_ref, hcat_ref, pg_ref, pv_ref, act_ref, acc_ref):
    tm = FFN_ROWS
    i = pl.program_id(0)
    j = pl.program_id(1)
    last = pl.num_programs(1) - 1

    def conv(p_ref, w_ref, cw_ref, cb_ref):
        p_ref[...] = _dot(hcat_ref[...], w_ref[...])
        u = cb_ref[...] + cw_ref[CONV_WIDTH - 1:CONV_WIDTH, :] * p_ref[HALO:HALO + tm, :]
        for tap in range(1, CONV_WIDTH):
            k = CONV_WIDTH - 1 - tap
            u = u + cw_ref[k:k + 1, :] * p_ref[HALO - tap:HALO - tap + tm, :]
        return u

    def build(slot):
        gate = conv(pg_ref, wg_ref, cwg_ref, cbg_ref)
        val = conv(pv_ref, wv_ref, cwv_ref, cbv_ref)
        act_ref[slot] = (jax.nn.gelu(gate, approximate=True) * val).astype(BF16)

    def multiply(slot):
        return _dot(act_ref[slot], wd_ref[...])

    @pl.when(j == 0)
    def _():
        first = (i % seq_tiles) == 0
        hcat_ref[0:HALO, :] = jnp.where(first, jnp.zeros_like(halo_ref[...]), halo_ref[...])
        hcat_ref[HALO:, :] = h_ref[...]
        acc_ref[...] = jnp.zeros_like(acc_ref)
        build(0)

    @pl.when(jnp.logical_and(j > 0, j < last))
    def _():
        slot = j % 2
        acc_ref[...] += multiply(1 - slot)
        build(slot)

    @pl.when(j == last)
    def _():
        y = acc_ref[...] + multiply((last - 1) % 2)
        o_ref[...] = x1_ref[...] + _rmsnorm(y, g_post_ref[...])


def _ffn(h2, x1, w_up_blocks, conv_w, conv_b, w_down, g_post, seq_len):
    n = h2.shape[0]
    tm, tn = FFN_ROWS, FFN_COLS
    nj = D_FF // tn
    halo_blocks = tm // HALO
    up_t = lambda j: jnp.minimum(j, nj - 1)
    return pl.pallas_call(
        functools.partial(_ffn_kernel, seq_len // tm),
        grid=(n // tm, nj + 1),
        in_specs=[pl.BlockSpec((tm, D_MODEL), lambda i, j: (i, 0)),
                  pl.BlockSpec((HALO, D_MODEL), lambda i, j: (jnp.maximum(i * halo_blocks - 1, 0), 0)),
                  pl.BlockSpec((tm, D_MODEL), lambda i, j: (i, 0)),
                  pl.BlockSpec((None, D_MODEL, tn), lambda i, j: (up_t(j), 0, 0)),
                  pl.BlockSpec((None, D_MODEL, tn), lambda i, j: (up_t(j) + nj, 0, 0)),
                  pl.BlockSpec((CONV_WIDTH, tn), lambda i, j: (0, up_t(j))),
                  pl.BlockSpec((CONV_WIDTH, tn), lambda i, j: (0, up_t(j) + nj)),
                  pl.BlockSpec((1, tn), lambda i, j: (0, up_t(j))),
                  pl.BlockSpec((1, tn), lambda i, j: (0, up_t(j) + nj)),
                  pl.BlockSpec((tn, D_MODEL), lambda i, j: (jnp.maximum(j - 1, 0), 0)),
                  pl.BlockSpec((1, D_MODEL), lambda i, j: (0, 0))],
        out_specs=pl.BlockSpec((tm, D_MODEL), lambda i, j: (i, 0)),
        out_shape=jax.ShapeDtypeStruct((n, D_MODEL), F32),
        scratch_shapes=[pltpu.VMEM((tm + HALO, D_MODEL), BF16), pltpu.VMEM((tm + HALO, tn), F32),
                        pltpu.VMEM((tm + HALO, tn), F32), pltpu.VMEM((2, tm, tn), BF16),
                        pltpu.VMEM((tm, D_MODEL), F32)],
        compiler_params=pltpu.CompilerParams(dimension_semantics=("arbitrary", "arbitrary"),
                                             vmem_limit_bytes=VMEM_LIMIT),
        name="ffn",
    )(h2, h2, x1, w_up_blocks, w_up_blocks, conv_w, conv_w, conv_b, conv_b, w_down, g_post)


def _swap_halves(w):
    half = w.shape[-1] // 2
    return jnp.concatenate([w[..., half:], w[..., :half]], axis=-1)


def _layout_w_kr(w_in):
    k_rope = w_in[:, _C_KR:_C_KR + MLA_ROPE_DIM]
    return jnp.concatenate([k_rope, _swap_halves(k_rope)], axis=1).astype(BF16)


def _layout_w_uq(w_uq):
    w = w_uq.reshape(MLA_Q_RANK, MLA_HEADS, MLA_QK_DIM)
    rope = w[:, :, MLA_NOPE_DIM:]
    w = jnp.concatenate([w[:, :, :MLA_NOPE_DIM], rope, _swap_halves(rope)], axis=-1)
    return w.reshape(MLA_Q_RANK, MLA_HEADS * MLA_QK_PAD).astype(BF16)


def _layout_w_ukv(w_ukv):
    w = w_ukv.reshape(MLA_KV_RANK, MLA_HEADS, MLA_NOPE_DIM + MLA_V_DIM)
    k_nope = w[:, :, :MLA_NOPE_DIM].reshape(MLA_KV_RANK, MLA_WIDTH)
    v = w[:, :, MLA_NOPE_DIM:].reshape(MLA_KV_RANK, MLA_WIDTH)
    return jnp.concatenate([k_nope, v], axis=1).astype(BF16)


def kernel(x, positions, g_attn_pre, w_in, g_cq, w_uq, g_ckv, w_ukv, g_out_sb, g_out_mla, w_o, g_attn_post,
           g_ffn_pre, w_up, conv_w, conv_b, w_down, g_ffn_post):
    b, s, d = x.shape
    depth = w_in.shape[0]
    n = b * s
    half = MLA_ROPE_DIM // 2
    inv_freq = ROPE_THETA ** (-jnp.arange(half, dtype=F32) / half)
    invf = jnp.tile(inv_freq, LANES // half)[None, :]
    pos2 = positions.reshape(n, 1)
    x2 = x.reshape(n, d)
    r2 = lambda g: g[None, :]
    for l in range(depth):
        qsb, ksb, vsb, qm, kn, kr, vm = _proj(
            x2, pos2, invf, r2(g_attn_pre[l]), w_in[l].astype(BF16), _layout_w_kr(w_in[l]), r2(g_cq[l]),
            _layout_w_uq(w_uq[l]), r2(g_ckv[l]), _layout_w_ukv(w_ukv[l]))
        b3 = lambda a: a.reshape(b, s, a.shape[-1])
        o_sb, o_mla, w_up_b, w_o_b, w_down_b = _attention(
            b3(qsb), b3(ksb), b3(vsb), b3(qm), b3(kn), b3(kr), b3(vm), [w_up[l], w_o[l], w_down[l]],
            [FFN_COLS, None, None])
        x1, h2 = _out_proj(o_sb.reshape(n, SB_WIDTH), o_mla.reshape(n, MLA_WIDTH), x2, r2(g_out_sb[l]),
                           r2(g_out_mla[l]), w_o_b, r2(g_attn_post[l]), r2(g_ffn_pre[l]))
        x2 = _ffn(h2, x1, w_up_b, conv_w[l], r2(conv_b[l]), w_down_b, r2(g_ffn_post[l]), s)
    return x2.reshape(b, s, d)
```

```python
import functools

import jax
import jax.numpy as jnp
from jax import lax
from jax.experimental import pallas as pl
from jax.experimental.pallas import tpu as pltpu

F32 = jnp.float32
BF16 = jnp.bfloat16

D_MODEL = 2048
CHUNK = 64
SB_HEADS = 8
SB_HEAD_DIM = 128
SB_WIDTH = SB_HEADS * SB_HEAD_DIM
MLA_HEADS = 8
MLA_NOPE_DIM = 128
MLA_ROPE_DIM = 64
MLA_V_DIM = 128
MLA_Q_RANK = 512
MLA_KV_RANK = 256
MLA_QK_DIM = MLA_NOPE_DIM + MLA_ROPE_DIM
MLA_WIDTH = MLA_HEADS * MLA_V_DIM
D_FF = 5632
CONV_WIDTH = 3
ROPE_THETA = 10000.0
EPS = 1e-6
LOG2_E = 1.4426950408889634

LANES = 128
SUBLANES = 8
MLA_QK_PAD = 2 * LANES

_C_QSB = 0
_C_KSB = SB_WIDTH
_C_VSB = 2 * SB_WIDTH
_C_CQ = 3 * SB_WIDTH
_C_CKV = _C_CQ + MLA_Q_RANK
_C_KR = _C_CKV + MLA_KV_RANK

PROJ_ROWS = 512
PROJ_CHUNK = 256
ATTN_TILE = 256
OUT_ROWS = 512
OUT_CHUNK = 256
FFN_ROWS = 512
FFN_COLS = 512

VMEM_LIMIT = 56 * 1024 * 1024


def _rmsnorm(x, g):
    return x * lax.rsqrt(jnp.mean(x * x, axis=-1, keepdims=True) + EPS) * g


def _resident(a):
    return pl.BlockSpec(a.shape, lambda *_: (0,) * a.ndim, pipeline_mode=pl.Buffered(1))


def _dot(a, b):
    return jnp.dot(a, b, preferred_element_type=F32)


def _dot_nt(a, b):
    return lax.dot_general(a, b, (((1,), (1,)), ((), ())), preferred_element_type=F32)


def _proj_kernel(x_ref, pos_ref, invf_ref, g_pre_ref, w_in_ref, w_kr_ref, g_cq_ref, w_uq_ref, g_ckv_ref, w_ukv_ref,
                 qsb_ref, ksb_ref, vsb_ref, qm_ref, kn_ref, kr_ref, vm_ref):
    sb_scale = SB_HEAD_DIM ** -0.5 * LOG2_E
    mla_scale = MLA_QK_DIM ** -0.5 * LOG2_E

    def wide(rs):
        h = _rmsnorm(x_ref[rs, :], g_pre_ref[...]).astype(BF16)

        def proj(lo, hi):
            return _dot(h, w_in_ref[:, lo:hi])

        cq = proj(_C_CQ, _C_CKV)
        ckv = proj(_C_CKV, _C_KR)
        kr = _dot(h, w_kr_ref[...])
        qsb_ref[rs, :] = (proj(_C_QSB, _C_KSB) * sb_scale).astype(BF16)
        ksb_ref[rs, :] = proj(_C_KSB, _C_VSB).astype(BF16)
        vsb_ref[rs, :] = proj(_C_VSB, _C_CQ).astype(BF16)
        return cq, ckv, kr

    def latent(rs, cq, ckv, kr):
        ang = pos_ref[rs, :].astype(F32) * invf_ref[...]
        lane = lax.broadcasted_iota(jnp.int32, ang.shape, 1)
        cos = jnp.cos(ang)
        sin = jnp.sin(ang)
        t1 = jnp.where(lane < MLA_ROPE_DIM, cos, 0.0)
        t2 = jnp.where(lane < MLA_ROPE_DIM // 2, -sin, jnp.where(lane < MLA_ROPE_DIM, sin, 0.0))

        def rope(g):
            return g * t1 + pltpu.roll(g, MLA_ROPE_DIM, axis=1) * t2

        kr_ref[rs, :] = rope(kr).astype(BF16)

        cq = _rmsnorm(cq, g_cq_ref[...]).astype(BF16)
        for hd in range(MLA_HEADS):
            c0 = hd * MLA_QK_PAD
            qh = _dot(cq, w_uq_ref[:, c0:c0 + MLA_QK_PAD])
            qm_ref[rs, c0:c0 + LANES] = (qh[:, :LANES] * mla_scale).astype(BF16)
            qm_ref[rs, c0 + LANES:c0 + MLA_QK_PAD] = (rope(qh[:, LANES:]) * mla_scale).astype(BF16)

        ckv = _rmsnorm(ckv, g_ckv_ref[...]).astype(BF16)
        kn_ref[rs, :] = _dot(ckv, w_ukv_ref[:, :MLA_WIDTH]).astype(BF16)
        vm_ref[rs, :] = _dot(ckv, w_ukv_ref[:, MLA_WIDTH:]).astype(BF16)

    pending = None
    for r in range(0, x_ref.shape[0], PROJ_CHUNK):
        rs = slice(r, r + PROJ_CHUNK)
        latents = wide(rs)
        if pending is not None:
            latent(*pending)
        pending = (rs,) + latents
    latent(*pending)


def _proj(x2, pos2, invf, g_pre, w_in_b, w_kr_p, g_cq, w_uq_p, g_ckv, w_ukv_p):
    n = x2.shape[0]
    tm = PROJ_ROWS
    row = lambda w: pl.BlockSpec((tm, w), lambda i: (i, 0))
    full = _resident
    out_widths = (SB_WIDTH, SB_WIDTH, SB_WIDTH, MLA_HEADS * MLA_QK_PAD, MLA_WIDTH, LANES, MLA_WIDTH)
    return pl.pallas_call(
        _proj_kernel,
        grid=(n // tm,),
        in_specs=[row(D_MODEL), row(1), full(invf), full(g_pre), full(w_in_b), full(w_kr_p), full(g_cq),
                  full(w_uq_p), full(g_ckv), full(w_ukv_p)],
        out_specs=[row(w) for w in out_widths],
        out_shape=[jax.ShapeDtypeStruct((n, w), BF16) for w in out_widths],
        compiler_params=pltpu.CompilerParams(dimension_semantics=("arbitrary",), vmem_limit_bytes=VMEM_LIMIT),
        name="proj",
    )(x2, pos2, invf, g_pre, w_in_b, w_kr_p, g_cq, w_uq_p, g_ckv, w_ukv_p)


def _softplus2(z):
    return jnp.maximum(z, 0.0) + jnp.log2(1.0 + jnp.exp2(-jnp.abs(z)))


def _cast_slabs(src_refs, dst_refs):
    for src_ref, dst_ref in zip(src_refs, dst_refs):
        if len(dst_ref.shape) == 3:
            width = dst_ref.shape[2]
            for c in range(dst_ref.shape[0]):
                dst_ref[c] = src_ref[:, c * width:(c + 1) * width].astype(BF16)
        else:
            dst_ref[...] = src_ref[...].astype(BF16)


def _slab_specs(weights, col_blocks, batch, heads):
    in_specs, out_specs, out_shapes = [], [], []
    for w, width in zip(weights, col_blocks):
        rows = w.shape[0] // (batch * heads)
        in_specs.append(pl.BlockSpec((rows, w.shape[1]), lambda bi, h: (bi * heads + h, 0)))
        if width is None:
            out_specs.append(pl.BlockSpec((rows, w.shape[1]), lambda bi, h: (bi * heads + h, 0)))
            out_shapes.append(jax.ShapeDtypeStruct(w.shape, BF16))
        else:
            n_blocks = w.shape[1] // width
            out_specs.append(pl.BlockSpec((n_blocks, rows, width), lambda bi, h: (0, bi * heads + h, 0)))
            out_shapes.append(jax.ShapeDtypeStruct((n_blocks, w.shape[0], width), BF16))
    return in_specs, out_specs, out_shapes


def _sb_pieces(q_ref, k_ref, v_ref, o_ref):
    t = ATTN_TILE
    row = lax.broadcasted_iota(jnp.int32, (t, t), 0)
    col = lax.broadcasted_iota(jnp.int32, (t, t), 1)
    mask = col < row
    tri = (row >= col).astype(BF16)
    tri2 = jnp.concatenate([tri, tri], axis=0)

    n_tiles = q_ref.shape[0] // t

    def logits(qi):
        return _dot_nt(q_ref[qi * t:(qi + 1) * t, :], k_ref[0:(qi + 1) * t, :])

    def suffix_sums(zj, diag):
        sp = _softplus2(zj)
        if diag:
            sp = jnp.where(mask, sp, 0.0)
        hi = sp.astype(BF16)
        lo = (sp - hi.astype(F32)).astype(BF16)
        return _dot(jnp.concatenate([hi, lo], axis=1), tri2)

    z = [logits(0)] + [None] * (n_tiles - 1)
    a_tiles = [[None] * (qi + 1) for qi in range(n_tiles)]
    carry = [None] * n_tiles

    def finish(qi, j, zj, local):
        cum = local if carry[qi] is None else local + carry[qi]
        a = jnp.exp2(zj - cum)
        if j == qi:
            a = jnp.where(mask, a, 0.0)
        a_tiles[qi][j] = a.astype(BF16)
        carry[qi] = local[:, 0:1] if carry[qi] is None else carry[qi] + local[:, 0:1]
        if j == 0:
            a_all = a_tiles[qi][0] if qi == 0 else jnp.concatenate(a_tiles[qi], axis=1)
            o_ref[qi * t:(qi + 1) * t, :] = _dot(a_all, v_ref[0:(qi + 1) * t, :])

    pending = None
    for qi in range(n_tiles):
        for j in range(qi, -1, -1):
            if j == qi and qi + 1 < n_tiles:
                z[qi + 1] = logits(qi + 1)
            zj = z[qi][:, j * t:(j + 1) * t]
            local = suffix_sums(zj, j == qi)
            if pending is not None:
                finish(*pending)
            pending = (qi, j, zj, local)
            yield
    finish(*pending)
    yield


def _mla_pieces(q_ref, kn_ref, kr_ref, v_ref, o_ref):
    t = ATTN_TILE
    row = lax.broadcasted_iota(jnp.int32, (t, t), 0)
    col = lax.broadcasted_iota(jnp.int32, (t, t), 1)
    mask = (col // CHUNK) <= (row // CHUNK)

    def scores(qi):
        n = qi + 1
        kcat = jnp.concatenate([kn_ref[0:n * t, :], kr_ref[0:n * t, :]], axis=1)
        return _dot_nt(q_ref[qi * t:n * t, :], kcat)

    n_tiles = q_ref.shape[0] // t
    sc_next = scores(0)
    for qi in range(n_tiles):
        n = qi + 1
        sc = sc_next
        if n < n_tiles:
            sc_next = scores(n)
        yield
        diag = jnp.where(mask, sc[:, qi * t:], -jnp.inf)
        sc = diag if n == 1 else jnp.concatenate([sc[:, :qi * t], diag], axis=1)
        p = jnp.exp2(sc - jnp.max(sc, axis=-1, keepdims=True))
        yield
        vcat = jnp.concatenate([v_ref[0:n * t, :], jnp.ones((n * t, LANES), BF16)], axis=1)
        acc = _dot(p.astype(BF16), vcat)
        o_ref[qi * t:n * t, :] = acc[:, :MLA_V_DIM] / acc[:, MLA_V_DIM:]
        yield


def _attn_kernel(n_cast, qs_ref, ks_ref, vs_ref, qm_ref, kn_ref, kr_ref, vm_ref, *refs):
    osb_ref, omla_ref = refs[n_cast], refs[n_cast + 1]
    _cast_slabs(refs[:n_cast], refs[n_cast + 2:])
    n_tiles = qs_ref.shape[0] // ATTN_TILE
    sb = _sb_pieces(qs_ref, ks_ref, vs_ref, osb_ref)
    mla = _mla_pieces(qm_ref, kn_ref, kr_ref, vm_ref, omla_ref)
    mla_per_tile = 3
    for qi in range(n_tiles):
        order = sorted([(k / (qi + 1), 0) for k in range(qi + 1)] +
                       [((k + 0.5) / mla_per_tile, 1) for k in range(mla_per_tile)])
        for _, stream in order:
            next(mla if stream else sb)
    for _ in sb:
        pass
    for _ in mla:
        pass


def _attention(qs, ks, vs, qm, kn, kr, vm, weights, col_blocks):
    b, s, _ = qs.shape
    head = lambda w: pl.BlockSpec((None, s, w), lambda bi, h: (bi, 0, h))
    slab_in, slab_out, slab_shapes = _slab_specs(weights, col_blocks, b, SB_HEADS)
    return pl.pallas_call(
        functools.partial(_attn_kernel, len(weights)),
        grid=(b, SB_HEADS),
        in_specs=[head(SB_HEAD_DIM)] * 3 +
                 [head(MLA_QK_PAD), head(MLA_NOPE_DIM), pl.BlockSpec((None, s, LANES), lambda bi, h: (bi, 0, 0)),
                  head(MLA_V_DIM)] + slab_in,
        out_specs=[head(SB_HEAD_DIM), head(MLA_V_DIM)] + slab_out,
        out_shape=[jax.ShapeDtypeStruct((b, s, SB_WIDTH), F32), jax.ShapeDtypeStruct((b, s, MLA_WIDTH), F32)] +
                  slab_shapes,
        compiler_params=pltpu.CompilerParams(dimension_semantics=("arbitrary",) * 2, vmem_limit_bytes=VMEM_LIMIT),
        name="attn",
    )(qs, ks, vs, qm, kn, kr, vm, *weights)


def _out_kernel(osb_ref, omla_ref, x_ref, g_sb_ref, g_mla_ref, w_o_ref, g_post_ref, g_ffn_ref, x1_ref, h2_ref):
    rows = osb_ref.shape[0]
    chunks = [slice(r, r + OUT_CHUNK) for r in range(0, rows, OUT_CHUNK)]

    def product(rs):
        n_sb = _rmsnorm(osb_ref[rs, :], g_sb_ref[...]).astype(BF16)
        n_mla = _rmsnorm(omla_ref[rs, :], g_mla_ref[...]).astype(BF16)
        return _dot(jnp.concatenate([n_sb, n_mla], axis=1), w_o_ref[...])

    def finish(rs, y):
        x1 = x_ref[rs, :] + _rmsnorm(y, g_post_ref[...])
        x1_ref[rs, :] = x1
        h2_ref[rs, :] = _rmsnorm(x1, g_ffn_ref[...]).astype(BF16)

    pending = None
    for rs in chunks:
        y = product(rs)
        if pending is not None:
            finish(*pending)
        pending = (rs, y)
    finish(*pending)


def _out_proj(o_sb, o_mla, x2, g_sb, g_mla, w_o, g_post, g_ffn):
    n = x2.shape[0]
    tm = OUT_ROWS
    row = lambda w: pl.BlockSpec((tm, w), lambda i: (i, 0))
    full = _resident
    return pl.pallas_call(
        _out_kernel,
        grid=(n // tm,),
        in_specs=[row(SB_WIDTH), row(MLA_WIDTH), row(D_MODEL), full(g_sb), full(g_mla), full(w_o), full(g_post),
                  full(g_ffn)],
        out_specs=[row(D_MODEL), row(D_MODEL)],
        out_shape=[jax.ShapeDtypeStruct((n, D_MODEL), F32), jax.ShapeDtypeStruct((n, D_MODEL), BF16)],
        compiler_params=pltpu.CompilerParams(dimension_semantics=("arbitrary",), vmem_limit_bytes=VMEM_LIMIT),
        name="out_proj",
    )(o_sb, o_mla, x2, g_sb, g_mla, w_o, g_post, g_ffn)


def _ffn_kernel(seq_tiles, h_ref, x1_ref, wg_ref, wv_ref, cwg_ref, cwv_ref, cbg_ref, cbv_ref, wd_ref,
                g_post_ref, o_ref, pg_ref, pv_ref, tail_ref, act_ref, acc_ref):
    tm = FFN_ROWS
    i = pl.program_id(0)
    j = pl.program_id(1)
    last = pl.num_programs(1) - 1
    first = (i % seq_tiles) == 0

    def conv(p_ref, which, w_ref, cw_ref, cb_ref):
        prev = tail_ref[j, which]
        p_ref[0:SUBLANES, :] = jnp.where(first, jnp.zeros_like(prev), prev)
        p_ref[SUBLANES:, :] = _dot(h_ref[...], w_ref[...])
        tail_ref[j, which] = p_ref[tm:tm + SUBLANES, :]
        u = cb_ref[...] + cw_ref[CONV_WIDTH - 1:CONV_WIDTH, :] * p_ref[SUBLANES:SUBLANES + tm, :]
        for tap in range(1, CONV_WIDTH):
            k = CONV_WIDTH - 1 - tap
            u = u + cw_ref[k:k + 1, :] * p_ref[SUBLANES - tap:SUBLANES - tap + tm, :]
        return u

    def build(slot):
        gate = conv(pg_ref, 0, wg_ref, cwg_ref, cbg_ref)
        val = conv(pv_ref, 1, wv_ref, cwv_ref, cbv_ref)
        act_ref[slot] = (jax.nn.gelu(gate, approximate=True) * val).astype(BF16)

    def multiply(slot):
        return _dot(act_ref[slot], wd_ref[...])

    @pl.when(jnp.logical_and(i == 0, j == 0))
    def _():
        tail_ref[...] = jnp.zeros_like(tail_ref)

    @pl.when(j == 0)
    def _():
        acc_ref[...] = jnp.zeros_like(acc_ref)
        build(0)

    @pl.when(jnp.logical_and(j > 0, j < last))
    def _():
        slot = j % 2
        acc_ref[...] += multiply(1 - slot)
        build(slot)

    @pl.when(j == last)
    def _():
        y = acc_ref[...] + multiply((last - 1) % 2)
        o_ref[...] = x1_ref[...] + _rmsnorm(y, g_post_ref[...])


def _ffn(h2, x1, w_up_blocks, conv_w, conv_b, w_down, g_post, seq_len):
    n = h2.shape[0]
    tm, tn = FFN_ROWS, FFN_COLS
    nj = D_FF // tn
    up_t = lambda j: jnp.minimum(j, nj - 1)
    return pl.pallas_call(
        functools.partial(_ffn_kernel, seq_len // tm),
        grid=(n // tm, nj + 1),
        in_specs=[pl.BlockSpec((tm, D_MODEL), lambda i, j: (i, 0)),
                  pl.BlockSpec((tm, D_MODEL), lambda i, j: (i, 0)),
                  pl.BlockSpec((None, D_MODEL, tn), lambda i, j: (up_t(j), 0, 0)),
                  pl.BlockSpec((None, D_MODEL, tn), lambda i, j: (up_t(j) + nj, 0, 0)),
                  pl.BlockSpec((CONV_WIDTH, tn), lambda i, j: (0, up_t(j))),
                  pl.BlockSpec((CONV_WIDTH, tn), lambda i, j: (0, up_t(j) + nj)),
                  pl.BlockSpec((1, tn), lambda i, j: (0, up_t(j))),
                  pl.BlockSpec((1, tn), lambda i, j: (0, up_t(j) + nj)),
                  pl.BlockSpec((tn, D_MODEL), lambda i, j: (jnp.maximum(j - 1, 0), 0)),
                  pl.BlockSpec((1, D_MODEL), lambda i, j: (0, 0))],
        out_specs=pl.BlockSpec((tm, D_MODEL), lambda i, j: (i, 0)),
        out_shape=jax.ShapeDtypeStruct((n, D_MODEL), F32),
        scratch_shapes=[pltpu.VMEM((tm + SUBLANES, tn), F32), pltpu.VMEM((tm + SUBLANES, tn), F32),
                        pltpu.VMEM((nj, 2, SUBLANES, tn), F32), pltpu.VMEM((2, tm, tn), BF16),
                        pltpu.VMEM((tm, D_MODEL), F32)],
        compiler_params=pltpu.CompilerParams(dimension_semantics=("arbitrary", "arbitrary"),
                                             vmem_limit_bytes=VMEM_LIMIT),
        name="ffn",
    )(h2, x1, w_up_blocks, w_up_blocks, conv_w, conv_w, conv_b, conv_b, w_down, g_post)


def _swap_halves(w):
    half = w.shape[-1] // 2
    return jnp.concatenate([w[..., half:], w[..., :half]], axis=-1)


def _layout_w_kr(w_in):
    k_rope = w_in[:, _C_KR:_C_KR + MLA_ROPE_DIM]
    return jnp.concatenate([k_rope, _swap_halves(k_rope)], axis=1).astype(BF16)


def _layout_w_uq(w_uq):
    w = w_uq.reshape(MLA_Q_RANK, MLA_HEADS, MLA_QK_DIM)
    rope = w[:, :, MLA_NOPE_DIM:]
    w = jnp.concatenate([w[:, :, :MLA_NOPE_DIM], rope, _swap_halves(rope)], axis=-1)
    return w.reshape(MLA_Q_RANK, MLA_HEADS * MLA_QK_PAD).astype(BF16)


def _layout_w_ukv(w_ukv):
    w = w_ukv.reshape(MLA_KV_RANK, MLA_HEADS, MLA_NOPE_DIM + MLA_V_DIM)
    k_nope = w[:, :, :MLA_NOPE_DIM].reshape(MLA_KV_RANK, MLA_WIDTH)
    v = w[:, :, MLA_NOPE_DIM:].reshape(MLA_KV_RANK, MLA_WIDTH)
    return jnp.concatenate([k_nope, v], axis=1).astype(BF16)


def kernel(x, positions, g_attn_pre, w_in, g_cq, w_uq, g_ckv, w_ukv, g_out_sb, g_out_mla, w_o, g_attn_post,
           g_ffn_pre, w_up, conv_w, conv_b, w_down, g_ffn_post):
    b, s, d = x.shape
    depth = w_in.shape[0]
    n = b * s
    half = MLA_ROPE_DIM // 2
    inv_freq = ROPE_THETA ** (-jnp.arange(half, dtype=F32) / half)
    invf = jnp.tile(inv_freq, LANES // half)[None, :]
    pos2 = positions.reshape(n, 1)
    x2 = x.reshape(n, d)
    r2 = lambda g: g[None, :]
    for l in range(depth):
        qsb, ksb, vsb, qm, kn, kr, vm = _proj(
            x2, pos2, invf, r2(g_attn_pre[l]), w_in[l].astype(BF16), _layout_w_kr(w_in[l]), r2(g_cq[l]),
            _layout_w_uq(w_uq[l]), r2(g_ckv[l]), _layout_w_ukv(w_ukv[l]))
        b3 = lambda a: a.reshape(b, s, a.shape[-1])
        o_sb, o_mla, w_up_b, w_o_b, w_down_b = _attention(
            b3(qsb), b3(ksb), b3(vsb), b3(qm), b3(kn), b3(kr), b3(vm), [w_up[l], w_o[l], w_down[l]],
            [FFN_COLS, None, None])
        x1, h2 = _out_proj(o_sb.reshape(n, SB_WIDTH), o_mla.reshape(n, MLA_WIDTH), x2, r2(g_out_sb[l]),
                           r2(g_out_mla[l]), w_o_b, r2(g_attn_post[l]), r2(g_ffn_pre[l]))
        x2 = _ffn(h2, x1, w_up_b, conv_w[l], r2(conv_b[l]), w_down_b, r2(g_ffn_post[l]), s)
    return x2.reshape(b, s, d)
```

```python
import functools

import jax
import jax.numpy as jnp
from jax import lax
from jax.experimental import pallas as pl
from jax.experimental.pallas import tpu as pltpu

F32 = jnp.float32
BF16 = jnp.bfloat16

D_MODEL = 2048
CHUNK = 64
SB_HEADS = 8
SB_HEAD_DIM = 128
SB_WIDTH = SB_HEADS * SB_HEAD_DIM
MLA_HEADS = 8
MLA_NOPE_DIM = 128
MLA_ROPE_DIM = 64
MLA_V_DIM = 128
MLA_Q_RANK = 512
MLA_KV_RANK = 256
MLA_QK_DIM = MLA_NOPE_DIM + MLA_ROPE_DIM
MLA_WIDTH = MLA_HEADS * MLA_V_DIM
D_FF = 5632
CONV_WIDTH = 3
ROPE_THETA = 10000.0
EPS = 1e-6
LOG2_E = 1.4426950408889634

LANES = 128
SUBLANES = 8
MLA_QK_PAD = 2 * LANES

_C_QSB = 0
_C_KSB = SB_WIDTH
_C_VSB = 2 * SB_WIDTH
_C_CQ = 3 * SB_WIDTH
_C_CKV = _C_CQ + MLA_Q_RANK
_C_KR = _C_CKV + MLA_KV_RANK

PROJ_ROWS = 512
PROJ_CHUNK = 256
ATTN_TILE = 256
OUT_ROWS = 512
OUT_CHUNK = 256
FFN_ROWS = 512
FFN_COLS = 512
FFN_SUB = 2

VMEM_LIMIT = 56 * 1024 * 1024


def _rmsnorm(x, g):
    return x * lax.rsqrt(jnp.mean(x * x, axis=-1, keepdims=True) + EPS) * g


def _resident(a):
    return pl.BlockSpec(a.shape, lambda *_: (0,) * a.ndim, pipeline_mode=pl.Buffered(1))


def _dot(a, b):
    return jnp.dot(a, b, preferred_element_type=F32)


def _dot_nt(a, b):
    return lax.dot_general(a, b, (((1,), (1,)), ((), ())), preferred_element_type=F32)


def _proj_kernel(x_ref, pos_ref, invf_ref, g_pre_ref, w_in_ref, w_kr_ref, g_cq_ref, w_uq_ref, g_ckv_ref, w_ukv_ref,
                 qsb_ref, ksb_ref, vsb_ref, qm_ref, kn_ref, kr_ref, vm_ref):
    sb_scale = SB_HEAD_DIM ** -0.5 * LOG2_E
    mla_scale = MLA_QK_DIM ** -0.5 * LOG2_E

    def wide(rs):
        h = _rmsnorm(x_ref[rs, :], g_pre_ref[...]).astype(BF16)

        def proj(lo, hi):
            return _dot(h, w_in_ref[:, lo:hi])

        cq = proj(_C_CQ, _C_CKV)
        ckv = proj(_C_CKV, _C_KR)
        kr = _dot(h, w_kr_ref[...])
        qsb_ref[rs, :] = (proj(_C_QSB, _C_KSB) * sb_scale).astype(BF16)
        ksb_ref[rs, :] = proj(_C_KSB, _C_VSB).astype(BF16)
        vsb_ref[rs, :] = proj(_C_VSB, _C_CQ).astype(BF16)
        return cq, ckv, kr

    def latent(rs, cq, ckv, kr):
        ang = pos_ref[rs, :].astype(F32) * invf_ref[...]
        lane = lax.broadcasted_iota(jnp.int32, ang.shape, 1)
        cos = jnp.cos(ang)
        sin = jnp.sin(ang)
        t1 = jnp.where(lane < MLA_ROPE_DIM, cos, 0.0)
        t2 = jnp.where(lane < MLA_ROPE_DIM // 2, -sin, jnp.where(lane < MLA_ROPE_DIM, sin, 0.0))

        def rope(g):
            return g * t1 + pltpu.roll(g, MLA_ROPE_DIM, axis=1) * t2

        kr_ref[rs, :] = rope(kr).astype(BF16)

        cq = _rmsnorm(cq, g_cq_ref[...]).astype(BF16)
        for hd in range(MLA_HEADS):
            c0 = hd * MLA_QK_PAD
            qh = _dot(cq, w_uq_ref[:, c0:c0 + MLA_QK_PAD])
            qm_ref[rs, c0:c0 + LANES] = (qh[:, :LANES] * mla_scale).astype(BF16)
            qm_ref[rs, c0 + LANES:c0 + MLA_QK_PAD] = (rope(qh[:, LANES:]) * mla_scale).astype(BF16)

        ckv = _rmsnorm(ckv, g_ckv_ref[...]).astype(BF16)
        kn_ref[rs, :] = _dot(ckv, w_ukv_ref[:, :MLA_WIDTH]).astype(BF16)
        vm_ref[rs, :] = _dot(ckv, w_ukv_ref[:, MLA_WIDTH:]).astype(BF16)

    pending = None
    for r in range(0, x_ref.shape[0], PROJ_CHUNK):
        rs = slice(r, r + PROJ_CHUNK)
        latents = wide(rs)
        if pending is not None:
            latent(*pending)
        pending = (rs,) + latents
    latent(*pending)


def _proj(x2, pos2, invf, g_pre, w_in_b, w_kr_p, g_cq, w_uq_p, g_ckv, w_ukv_p):
    n = x2.shape[0]
    tm = PROJ_ROWS
    row = lambda w: pl.BlockSpec((tm, w), lambda i: (i, 0))
    full = _resident
    out_widths = (SB_WIDTH, SB_WIDTH, SB_WIDTH, MLA_HEADS * MLA_QK_PAD, MLA_WIDTH, LANES, MLA_WIDTH)
    return pl.pallas_call(
        _proj_kernel,
        grid=(n // tm,),
        in_specs=[row(D_MODEL), row(1), full(invf), full(g_pre), full(w_in_b), full(w_kr_p), full(g_cq),
                  full(w_uq_p), full(g_ckv), full(w_ukv_p)],
        out_specs=[row(w) for w in out_widths],
        out_shape=[jax.ShapeDtypeStruct((n, w), BF16) for w in out_widths],
        compiler_params=pltpu.CompilerParams(dimension_semantics=("arbitrary",), vmem_limit_bytes=VMEM_LIMIT),
        name="proj",
    )(x2, pos2, invf, g_pre, w_in_b, w_kr_p, g_cq, w_uq_p, g_ckv, w_ukv_p)


def _softplus2(z):
    return jnp.maximum(z, 0.0) + jnp.log2(1.0 + jnp.exp2(-jnp.abs(z)))


def _cast_slabs(src_refs, dst_refs):
    for src_ref, dst_ref in zip(src_refs, dst_refs):
        if len(dst_ref.shape) == 3:
            width = dst_ref.shape[2]
            for c in range(dst_ref.shape[0]):
                dst_ref[c] = src_ref[:, c * width:(c + 1) * width].astype(BF16)
        else:
            dst_ref[...] = src_ref[...].astype(BF16)


def _slab_specs(weights, col_blocks, batch, heads):
    in_specs, out_specs, out_shapes = [], [], []
    for w, width in zip(weights, col_blocks):
        rows = w.shape[0] // (batch * heads)
        in_specs.append(pl.BlockSpec((rows, w.shape[1]), lambda bi, h: (bi * heads + h, 0)))
        if width is None:
            out_specs.append(pl.BlockSpec((rows, w.shape[1]), lambda bi, h: (bi * heads + h, 0)))
            out_shapes.append(jax.ShapeDtypeStruct(w.shape, BF16))
        else:
            n_blocks = w.shape[1] // width
            out_specs.append(pl.BlockSpec((n_blocks, rows, width), lambda bi, h: (0, bi * heads + h, 0)))
            out_shapes.append(jax.ShapeDtypeStruct((n_blocks, w.shape[0], width), BF16))
    return in_specs, out_specs, out_shapes


def _sb_pieces(q_ref, k_ref, v_ref, o_ref):
    t = ATTN_TILE
    row = lax.broadcasted_iota(jnp.int32, (t, t), 0)
    col = lax.broadcasted_iota(jnp.int32, (t, t), 1)
    mask = col < row
    tri = (row >= col).astype(BF16)
    tri2 = jnp.concatenate([tri, tri], axis=0)

    n_tiles = q_ref.shape[0] // t

    def logits(qi):
        return _dot_nt(q_ref[qi * t:(qi + 1) * t, :], k_ref[0:(qi + 1) * t, :])

    def suffix_sums(zj, diag):
        sp = _softplus2(zj)
        if diag:
            sp = jnp.where(mask, sp, 0.0)
        hi = sp.astype(BF16)
        lo = (sp - hi.astype(F32)).astype(BF16)
        return _dot(jnp.concatenate([hi, lo], axis=1), tri2)

    z = [logits(0)] + [None] * (n_tiles - 1)
    a_tiles = [[None] * (qi + 1) for qi in range(n_tiles)]
    carry = [None] * n_tiles

    def finish(qi, j, zj, local):
        cum = local if carry[qi] is None else local + carry[qi]
        a = jnp.exp2(zj - cum)
        if j == qi:
            a = jnp.where(mask, a, 0.0)
        a_tiles[qi][j] = a.astype(BF16)
        carry[qi] = local[:, 0:1] if carry[qi] is None else carry[qi] + local[:, 0:1]
        if j == 0:
            a_all = a_tiles[qi][0] if qi == 0 else jnp.concatenate(a_tiles[qi], axis=1)
            o_ref[qi * t:(qi + 1) * t, :] = _dot(a_all, v_ref[0:(qi + 1) * t, :])

    pending = None
    for qi in range(n_tiles):
        for j in range(qi, -1, -1):
            if j == qi and qi + 1 < n_tiles:
                z[qi + 1] = logits(qi + 1)
            zj = z[qi][:, j * t:(j + 1) * t]
            local = suffix_sums(zj, j == qi)
            if pending is not None:
                finish(*pending)
            pending = (qi, j, zj, local)
            yield
    finish(*pending)
    yield


def _mla_pieces(q_ref, kn_ref, kr_ref, v_ref, o_ref):
    t = ATTN_TILE
    row = lax.broadcasted_iota(jnp.int32, (t, t), 0)
    col = lax.broadcasted_iota(jnp.int32, (t, t), 1)
    mask = (col // CHUNK) <= (row // CHUNK)

    def scores(qi):
        n = qi + 1
        kcat = jnp.concatenate([kn_ref[0:n * t, :], kr_ref[0:n * t, :]], axis=1)
        return _dot_nt(q_ref[qi * t:n * t, :], kcat)

    n_tiles = q_ref.shape[0] // t
    sc_next = scores(0)
    for qi in range(n_tiles):
        n = qi + 1
        sc = sc_next
        if n < n_tiles:
            sc_next = scores(n)
        yield
        diag = jnp.where(mask, sc[:, qi * t:], -jnp.inf)
        sc = diag if n == 1 else jnp.concatenate([sc[:, :qi * t], diag], axis=1)
        p = jnp.exp2(sc - jnp.max(sc, axis=-1, keepdims=True))
        yield
        vcat = jnp.concatenate([v_ref[0:n * t, :], jnp.ones((n * t, LANES), BF16)], axis=1)
        acc = _dot(p.astype(BF16), vcat)
        o_ref[qi * t:n * t, :] = acc[:, :MLA_V_DIM] / acc[:, MLA_V_DIM:]
        yield


def _attn_kernel(n_cast, qs_ref, ks_ref, vs_ref, qm_ref, kn_ref, kr_ref, vm_ref, *refs):
    osb_ref, omla_ref = refs[n_cast], refs[n_cast + 1]
    _cast_slabs(refs[:n_cast], refs[n_cast + 2:])
    n_tiles = qs_ref.shape[0] // ATTN_TILE
    sb = _sb_pieces(qs_ref, ks_ref, vs_ref, osb_ref)
    mla = _mla_pieces(qm_ref, kn_ref, kr_ref, vm_ref, omla_ref)
    mla_per_tile = 3
    for qi in range(n_tiles):
        order = sorted([(k / (qi + 1), 0) for k in range(qi + 1)] +
                       [((k + 0.5) / mla_per_tile, 1) for k in range(mla_per_tile)])
        for _, stream in order:
            next(mla if stream else sb)
    for _ in sb:
        pass
    for _ in mla:
        pass


def _attention(qs, ks, vs, qm, kn, kr, vm, weights, col_blocks):
    b, s, _ = qs.shape
    head = lambda w: pl.BlockSpec((None, s, w), lambda bi, h: (bi, 0, h))
    slab_in, slab_out, slab_shapes = _slab_specs(weights, col_blocks, b, SB_HEADS)
    return pl.pallas_call(
        functools.partial(_attn_kernel, len(weights)),
        grid=(b, SB_HEADS),
        in_specs=[head(SB_HEAD_DIM)] * 3 +
                 [head(MLA_QK_PAD), head(MLA_NOPE_DIM), pl.BlockSpec((None, s, LANES), lambda bi, h: (bi, 0, 0)),
                  head(MLA_V_DIM)] + slab_in,
        out_specs=[head(SB_HEAD_DIM), head(MLA_V_DIM)] + slab_out,
        out_shape=[jax.ShapeDtypeStruct((b, s, SB_WIDTH), F32), jax.ShapeDtypeStruct((b, s, MLA_WIDTH), F32)] +
                  slab_shapes,
        compiler_params=pltpu.CompilerParams(dimension_semantics=("arbitrary",) * 2, vmem_limit_bytes=VMEM_LIMIT),
        name="attn",
    )(qs, ks, vs, qm, kn, kr, vm, *weights)


def _out_kernel(osb_ref, omla_ref, x_ref, g_sb_ref, g_mla_ref, w_o_ref, g_post_ref, g_ffn_ref, x1_ref, h2_ref):
    rows = osb_ref.shape[0]
    chunks = [slice(r, r + OUT_CHUNK) for r in range(0, rows, OUT_CHUNK)]

    def product(rs):
        n_sb = _rmsnorm(osb_ref[rs, :], g_sb_ref[...]).astype(BF16)
        n_mla = _rmsnorm(omla_ref[rs, :], g_mla_ref[...]).astype(BF16)
        return _dot(jnp.concatenate([n_sb, n_mla], axis=1), w_o_ref[...])

    def finish(rs, y):
        x1 = x_ref[rs, :] + _rmsnorm(y, g_post_ref[...])
        x1_ref[rs, :] = x1
        h2_ref[rs, :] = _rmsnorm(x1, g_ffn_ref[...]).astype(BF16)

    pending = None
    for rs in chunks:
        y = product(rs)
        if pending is not None:
            finish(*pending)
        pending = (rs, y)
    finish(*pending)


def _out_proj(o_sb, o_mla, x2, g_sb, g_mla, w_o, g_post, g_ffn):
    n = x2.shape[0]
    tm = OUT_ROWS
    row = lambda w: pl.BlockSpec((tm, w), lambda i: (i, 0))
    full = _resident
    return pl.pallas_call(
        _out_kernel,
        grid=(n // tm,),
        in_specs=[row(SB_WIDTH), row(MLA_WIDTH), row(D_MODEL), full(g_sb), full(g_mla), full(w_o), full(g_post),
                  full(g_ffn)],
        out_specs=[row(D_MODEL), row(D_MODEL)],
        out_shape=[jax.ShapeDtypeStruct((n, D_MODEL), F32), jax.ShapeDtypeStruct((n, D_MODEL), BF16)],
        compiler_params=pltpu.CompilerParams(dimension_semantics=("arbitrary",), vmem_limit_bytes=VMEM_LIMIT),
        name="out_proj",
    )(o_sb, o_mla, x2, g_sb, g_mla, w_o, g_post, g_ffn)


def _ffn_kernel(seq_tiles, h_ref, x1_ref, wg_ref, wv_ref, cwg_ref, cwv_ref, cbg_ref, cbv_ref, wd_ref,
                g_post_ref, o_ref, pg_ref, pv_ref, tail_ref, act_ref, acc_ref):
    tm = FFN_ROWS
    i = pl.program_id(0)
    s = pl.program_id(1)
    last = pl.num_programs(1) - 1
    j, r = s // FFN_SUB, s % FFN_SUB
    r_mul = (s + FFN_SUB - 1) % FFN_SUB
    rows = pl.ds(pl.multiple_of(r * tm, tm), tm)
    first = ((i * FFN_SUB + r) % seq_tiles) == 0

    def conv(p_ref, which, w_ref, cw_ref, cb_ref):
        prev = tail_ref[j, which]
        p_ref[0:SUBLANES, :] = jnp.where(first, jnp.zeros_like(prev), prev)
        p_ref[SUBLANES:, :] = _dot(h_ref[rows, :], w_ref[...])
        tail_ref[j, which] = p_ref[tm:tm + SUBLANES, :]
        u = cb_ref[...] + cw_ref[CONV_WIDTH - 1:CONV_WIDTH, :] * p_ref[SUBLANES:SUBLANES + tm, :]
        for tap in range(1, CONV_WIDTH):
            k = CONV_WIDTH - 1 - tap
            u = u + cw_ref[k:k + 1, :] * p_ref[SUBLANES - tap:SUBLANES - tap + tm, :]
        return u

    def build(slot):
        gate = conv(pg_ref, 0, wg_ref, cwg_ref, cbg_ref)
        val = conv(pv_ref, 1, wv_ref, cwv_ref, cbv_ref)
        act_ref[slot] = (jax.nn.gelu(gate, approximate=True) * val).astype(BF16)

    def multiply(slot):
        return _dot(act_ref[slot], wd_ref[...])

    @pl.when(jnp.logical_and(i == 0, s == 0))
    def _():
        tail_ref[...] = jnp.zeros_like(tail_ref)

    @pl.when(j == 0)
    def _():
        acc_ref[r] = jnp.zeros(acc_ref.shape[1:], F32)

    @pl.when(s == 0)
    def _():
        build(0)

    @pl.when(jnp.logical_and(s > 0, s < last))
    def _():
        slot = s % 2
        acc_ref[r_mul] += multiply(1 - slot)
        build(slot)

    @pl.when(s == last)
    def _():
        acc_ref[r_mul] += multiply((last - 1) % 2)

    @pl.when(s > last - FFN_SUB)
    def _():
        o_ref[...] = x1_ref[...] + _rmsnorm(acc_ref[r_mul], g_post_ref[...])


def _ffn(h2, x1, w_up_blocks, conv_w, conv_b, w_down, g_post, seq_len):
    n = h2.shape[0]
    tm, tn = FFN_ROWS, FFN_COLS
    nj = D_FF // tn
    sub = FFN_SUB
    steps = nj * sub
    up_t = lambda s: jnp.minimum(s // sub, nj - 1)
    down_t = lambda s: jnp.clip((s - 1) // sub, 0, nj - 1)
    done = lambda i, s: (i * sub + jnp.clip(s - (steps - sub + 1), 0, sub - 1), 0)
    return pl.pallas_call(
        functools.partial(_ffn_kernel, seq_len // tm),
        grid=(n // (sub * tm), steps + 1),
        in_specs=[pl.BlockSpec((sub * tm, D_MODEL), lambda i, s: (i, 0)),
                  pl.BlockSpec((tm, D_MODEL), done),
                  pl.BlockSpec((None, D_MODEL, tn), lambda i, s: (up_t(s), 0, 0)),
                  pl.BlockSpec((None, D_MODEL, tn), lambda i, s: (up_t(s) + nj, 0, 0)),
                  pl.BlockSpec((CONV_WIDTH, tn), lambda i, s: (0, up_t(s))),
                  pl.BlockSpec((CONV_WIDTH, tn), lambda i, s: (0, up_t(s) + nj)),
                  pl.BlockSpec((1, tn), lambda i, s: (0, up_t(s))),
                  pl.BlockSpec((1, tn), lambda i, s: (0, up_t(s) + nj)),
                  pl.BlockSpec((tn, D_MODEL), lambda i, s: (down_t(s), 0)),
                  pl.BlockSpec((1, D_MODEL), lambda i, s: (0, 0))],
        out_specs=pl.BlockSpec((tm, D_MODEL), done),
        out_shape=jax.ShapeDtypeStruct((n, D_MODEL), F32),
        scratch_shapes=[pltpu.VMEM((tm + SUBLANES, tn), F32), pltpu.VMEM((tm + SUBLANES, tn), F32),
                        pltpu.VMEM((nj, 2, SUBLANES, tn), F32), pltpu.VMEM((2, tm, tn), BF16),
                        pltpu.VMEM((sub, tm, D_MODEL), F32)],
        compiler_params=pltpu.CompilerParams(dimension_semantics=("arbitrary", "arbitrary"),
                                             vmem_limit_bytes=VMEM_LIMIT),
        name="ffn",
    )(h2, x1, w_up_blocks, w_up_blocks, conv_w, conv_w, conv_b, conv_b, w_down, g_post)


def _swap_halves(w):
    half = w.shape[-1] // 2
    return jnp.concatenate([w[..., half:], w[..., :half]], axis=-1)


def _layout_w_kr(w_in):
    k_rope = w_in[:, _C_KR:_C_KR + MLA_ROPE_DIM]
    return jnp.concatenate([k_rope, _swap_halves(k_rope)], axis=1).astype(BF16)


def _layout_w_uq(w_uq):
    w = w_uq.reshape(MLA_Q_RANK, MLA_HEADS, MLA_QK_DIM)
    rope = w[:, :, MLA_NOPE_DIM:]
    w = jnp.concatenate([w[:, :, :MLA_NOPE_DIM], rope, _swap_halves(rope)], axis=-1)
    return w.reshape(MLA_Q_RANK, MLA_HEADS * MLA_QK_PAD).astype(BF16)


def _layout_w_ukv(w_ukv):
    w = w_ukv.reshape(MLA_KV_RANK, MLA_HEADS, MLA_NOPE_DIM + MLA_V_DIM)
    k_nope = w[:, :, :MLA_NOPE_DIM].reshape(MLA_KV_RANK, MLA_WIDTH)
    v = w[:, :, MLA_NOPE_DIM:].reshape(MLA_KV_RANK, MLA_WIDTH)
    return jnp.concatenate([k_nope, v], axis=1).astype(BF16)


def kernel(x, positions, g_attn_pre, w_in, g_cq, w_uq, g_ckv, w_ukv, g_out_sb, g_out_mla, w_o, g_attn_post,
           g_ffn_pre, w_up, conv_w, conv_b, w_down, g_ffn_post):
    b, s, d = x.shape
    depth = w_in.shape[0]
    n = b * s
    half = MLA_ROPE_DIM // 2
    inv_freq = ROPE_THETA ** (-jnp.arange(half, dtype=F32) / half)
    invf = jnp.tile(inv_freq, LANES // half)[None, :]
    pos2 = positions.reshape(n, 1)
    x2 = x.reshape(n, d)
    r2 = lambda g: g[None, :]
    for l in range(depth):
        qsb, ksb, vsb, qm, kn, kr, vm = _proj(
            x2, pos2, invf, r2(g_attn_pre[l]), w_in[l].astype(BF16), _layout_w_kr(w_in[l]), r2(g_cq[l]),
            _layout_w_uq(w_uq[l]), r2(g_ckv[l]), _layout_w_ukv(w_ukv[l]))
        b3 = lambda a: a.reshape(b, s, a.shape[-1])
        o_sb, o_mla, w_up_b, w_o_b, w_down_b = _attention(
            b3(qsb), b3(ksb), b3(vsb), b3(qm), b3(kn), b3(kr), b3(vm), [w_up[l], w_o[l], w_down[l]],
            [FFN_COLS, None, None])
        x1, h2 = _out_proj(o_sb.reshape(n, SB_WIDTH), o_mla.reshape(n, MLA_WIDTH), x2, r2(g_out_sb[l]),
                           r2(g_out_mla[l]), w_o_b, r2(g_attn_post[l]), r2(g_ffn_pre[l]))
        x2 = _ffn(h2, x1, w_up_b, conv_w[l], r2(conv_b[l]), w_down_b, r2(g_ffn_post[l]), s)
    return x2.reshape(b, s, d)
```

```python
import functools

import jax
import jax.numpy as jnp
from jax import lax
from jax.experimental import pallas as pl
from jax.experimental.pallas import tpu as pltpu

F32 = jnp.float32
BF16 = jnp.bfloat16

D_MODEL = 2048
CHUNK = 64
SB_HEADS = 8
SB_HEAD_DIM = 128
SB_WIDTH = SB_HEADS * SB_HEAD_DIM
MLA_HEADS = 8
MLA_NOPE_DIM = 128
MLA_ROPE_DIM = 64
MLA_V_DIM = 128
MLA_Q_RANK = 512
MLA_KV_RANK = 256
MLA_QK_DIM = MLA_NOPE_DIM + MLA_ROPE_DIM
MLA_WIDTH = MLA_HEADS * MLA_V_DIM
D_FF = 5632
CONV_WIDTH = 3
ROPE_THETA = 10000.0
EPS = 1e-6
LOG2_E = 1.4426950408889634

LANES = 128
SUBLANES = 8
MLA_QK_PAD = 2 * LANES

_C_QSB = 0
_C_KSB = SB_WIDTH
_C_VSB = 2 * SB_WIDTH
_C_CQ = 3 * SB_WIDTH
_C_CKV = _C_CQ + MLA_Q_RANK
_C_KR = _C_CKV + MLA_KV_RANK

PROJ_ROWS = 512
PROJ_CHUNK = 256
ATTN_TILE = 256
OUT_ROWS = 512
OUT_CHUNK = 256
FFN_ROWS = 512
FFN_COLS = 512

VMEM_LIMIT = 56 * 1024 * 1024


def _rmsnorm(x, g):
    return x * lax.rsqrt(jnp.mean(x * x, axis=-1, keepdims=True) + EPS) * g


def _resident(a):
    return pl.BlockSpec(a.shape, lambda *_: (0,) * a.ndim, pipeline_mode=pl.Buffered(1))


def _dot(a, b):
    return jnp.dot(a, b, preferred_element_type=F32)


def _dot_nt(a, b):
    return lax.dot_general(a, b, (((1,), (1,)), ((), ())), preferred_element_type=F32)


def _proj_kernel(x_ref, pos_ref, invf_ref, g_pre_ref, w_in_ref, w_kr_ref, g_cq_ref, w_uq_ref, g_ckv_ref, w_ukv_ref,
                 qsb_ref, ksb_ref, vsb_ref, qm_ref, kn_ref, kr_ref, vm_ref):
    sb_scale = SB_HEAD_DIM ** -0.5 * LOG2_E
    mla_scale = MLA_QK_DIM ** -0.5 * LOG2_E

    def wide(rs):
        h = _rmsnorm(x_ref[rs, :], g_pre_ref[...]).astype(BF16)

        def proj(lo, hi):
            return _dot(h, w_in_ref[:, lo:hi])

        cq = proj(_C_CQ, _C_CKV)
        ckv = proj(_C_CKV, _C_KR)
        kr = _dot(h, w_kr_ref[...])
        qsb_ref[rs, :] = (proj(_C_QSB, _C_KSB) * sb_scale).astype(BF16)
        ksb_ref[rs, :] = proj(_C_KSB, _C_VSB).astype(BF16)
        vsb_ref[rs, :] = proj(_C_VSB, _C_CQ).astype(BF16)
        return cq, ckv, kr

    def latent(rs, cq, ckv, kr):
        freqs = MLA_ROPE_DIM // 2
        groups = LANES // freqs
        part = (rs.stop - rs.start) // groups
        pos = pos_ref[rs, :].astype(F32)
        lane = lax.broadcasted_iota(jnp.int32, (part, LANES), 1)
        pos_by_group = pos[0:part]
        for g in range(1, groups):
            pos_by_group = jnp.where(lane >= g * freqs, pos[g * part:(g + 1) * part], pos_by_group)
        ang = pos_by_group * invf_ref[...]

        def spread(table):
            parts = []
            for g in range(groups):
                own = table if g == 0 else pltpu.roll(table, LANES - g * freqs, axis=1)
                twice = jnp.where(lane < freqs, own, pltpu.roll(own, freqs, axis=1))
                parts.append(jnp.where(lane < MLA_ROPE_DIM, twice, 0.0))
            return jnp.concatenate(parts, axis=0)

        t1 = spread(jnp.cos(ang))
        sin = spread(jnp.sin(ang))
        t2 = jnp.where(lax.broadcasted_iota(jnp.int32, sin.shape, 1) < freqs, -sin, sin)

        def rope(g):
            return g * t1 + pltpu.roll(g, MLA_ROPE_DIM, axis=1) * t2

        kr_ref[rs, :] = rope(kr).astype(BF16)

        cq = _rmsnorm(cq, g_cq_ref[...]).astype(BF16)
        for hd in range(MLA_HEADS):
            c0 = hd * MLA_QK_PAD
            qh = _dot(cq, w_uq_ref[:, c0:c0 + MLA_QK_PAD])
            qm_ref[rs, c0:c0 + LANES] = (qh[:, :LANES] * mla_scale).astype(BF16)
            qm_ref[rs, c0 + LANES:c0 + MLA_QK_PAD] = (rope(qh[:, LANES:]) * mla_scale).astype(BF16)

        ckv = _rmsnorm(ckv, g_ckv_ref[...]).astype(BF16)
        kn_ref[rs, :] = _dot(ckv, w_ukv_ref[:, :MLA_WIDTH]).astype(BF16)
        vm_ref[rs, :] = _dot(ckv, w_ukv_ref[:, MLA_WIDTH:]).astype(BF16)

    pending = None
    for r in range(0, x_ref.shape[0], PROJ_CHUNK):
        rs = slice(r, r + PROJ_CHUNK)
        latents = wide(rs)
        if pending is not None:
            latent(*pending)
        pending = (rs,) + latents
    latent(*pending)


def _proj(x2, pos2, invf, g_pre, w_in_b, w_kr_p, g_cq, w_uq_p, g_ckv, w_ukv_p):
    n = x2.shape[0]
    tm = PROJ_ROWS
    row = lambda w: pl.BlockSpec((tm, w), lambda i: (i, 0))
    full = _resident
    out_widths = (SB_WIDTH, SB_WIDTH, SB_WIDTH, MLA_HEADS * MLA_QK_PAD, MLA_WIDTH, LANES, MLA_WIDTH)
    return pl.pallas_call(
        _proj_kernel,
        grid=(n // tm,),
        in_specs=[row(D_MODEL), row(1), full(invf), full(g_pre), full(w_in_b), full(w_kr_p), full(g_cq),
                  full(w_uq_p), full(g_ckv), full(w_ukv_p)],
        out_specs=[row(w) for w in out_widths],
        out_shape=[jax.ShapeDtypeStruct((n, w), BF16) for w in out_widths],
        compiler_params=pltpu.CompilerParams(dimension_semantics=("arbitrary",), vmem_limit_bytes=VMEM_LIMIT),
        name="proj",
    )(x2, pos2, invf, g_pre, w_in_b, w_kr_p, g_cq, w_uq_p, g_ckv, w_ukv_p)


def _softplus2(z):
    return jnp.maximum(z, 0.0) + jnp.log2(1.0 + jnp.exp2(-jnp.abs(z)))


def _cast_slabs(src_refs, dst_refs):
    for src_ref, dst_ref in zip(src_refs, dst_refs):
        if len(dst_ref.shape) == 3:
            width = dst_ref.shape[2]
            for c in range(dst_ref.shape[0]):
                dst_ref[c] = src_ref[:, c * width:(c + 1) * width].astype(BF16)
        else:
            dst_ref[...] = src_ref[...].astype(BF16)


def _slab_specs(weights, col_blocks, batch, heads):
    in_specs, out_specs, out_shapes = [], [], []
    for w, width in zip(weights, col_blocks):
        rows = w.shape[0] // (batch * heads)
        in_specs.append(pl.BlockSpec((rows, w.shape[1]), lambda bi, h: (bi * heads + h, 0)))
        if width is None:
            out_specs.append(pl.BlockSpec((rows, w.shape[1]), lambda bi, h: (bi * heads + h, 0)))
            out_shapes.append(jax.ShapeDtypeStruct(w.shape, BF16))
        else:
            n_blocks = w.shape[1] // width
            out_specs.append(pl.BlockSpec((n_blocks, rows, width), lambda bi, h: (0, bi * heads + h, 0)))
            out_shapes.append(jax.ShapeDtypeStruct((n_blocks, w.shape[0], width), BF16))
    return in_specs, out_specs, out_shapes


def _sb_pieces(q_ref, k_ref, v_ref, o_ref):
    t = ATTN_TILE
    row = lax.broadcasted_iota(jnp.int32, (t, t), 0)
    col = lax.broadcasted_iota(jnp.int32, (t, t), 1)
    mask = col < row
    tri = (row >= col).astype(BF16)
    tri2 = jnp.concatenate([tri, tri], axis=0)

    n_tiles = q_ref.shape[0] // t

    def logits(qi):
        return _dot_nt(q_ref[qi * t:(qi + 1) * t, :], k_ref[0:(qi + 1) * t, :])

    def suffix_sums(zj, diag):
        sp = _softplus2(zj)
        if diag:
            sp = jnp.where(mask, sp, 0.0)
        hi = sp.astype(BF16)
        lo = (sp - hi.astype(F32)).astype(BF16)
        return _dot(jnp.concatenate([hi, lo], axis=1), tri2)

    z = [logits(0)] + [None] * (n_tiles - 1)
    a_tiles = [[None] * (qi + 1) for qi in range(n_tiles)]
    carry = [None] * n_tiles

    def finish(qi, j, zj, local):
        cum = local if carry[qi] is None else local + carry[qi]
        a = jnp.exp2(zj - cum)
        if j == qi:
            a = jnp.where(mask, a, 0.0)
        a_tiles[qi][j] = a.astype(BF16)
        carry[qi] = local[:, 0:1] if carry[qi] is None else carry[qi] + local[:, 0:1]
        if j == 0:
            a_all = a_tiles[qi][0] if qi == 0 else jnp.concatenate(a_tiles[qi], axis=1)
            o_ref[qi * t:(qi + 1) * t, :] = _dot(a_all, v_ref[0:(qi + 1) * t, :])

    pending = None
    for qi in range(n_tiles):
        for j in range(qi, -1, -1):
            if j == qi and qi + 1 < n_tiles:
                z[qi + 1] = logits(qi + 1)
            zj = z[qi][:, j * t:(j + 1) * t]
            local = suffix_sums(zj, j == qi)
            if pending is not None:
                finish(*pending)
            pending = (qi, j, zj, local)
            yield
    finish(*pending)
    yield


def _mla_pieces(q_ref, kn_ref, kr_ref, v_ref, o_ref):
    t = ATTN_TILE
    row = lax.broadcasted_iota(jnp.int32, (t, t), 0)
    col = lax.broadcasted_iota(jnp.int32, (t, t), 1)
    mask = (col // CHUNK) <= (row // CHUNK)

    def scores(qi):
        n = qi + 1
        kcat = jnp.concatenate([kn_ref[0:n * t, :], kr_ref[0:n * t, :]], axis=1)
        return _dot_nt(q_ref[qi * t:n * t, :], kcat)

    n_tiles = q_ref.shape[0] // t
    sc_next = scores(0)
    for qi in range(n_tiles):
        n = qi + 1
        sc = sc_next
        if n < n_tiles:
            sc_next = scores(n)
        yield
        diag = jnp.where(mask, sc[:, qi * t:], -jnp.inf)
        sc = diag if n == 1 else jnp.concatenate([sc[:, :qi * t], diag], axis=1)
        p = jnp.exp2(sc - jnp.max(sc, axis=-1, keepdims=True))
        yield
        vcat = jnp.concatenate([v_ref[0:n * t, :], jnp.ones((n * t, LANES), BF16)], axis=1)
        acc = _dot(p.astype(BF16), vcat)
        o_ref[qi * t:n * t, :] = acc[:, :MLA_V_DIM] / acc[:, MLA_V_DIM:]
        yield


def _attn_kernel(n_cast, qs_ref, ks_ref, vs_ref, qm_ref, kn_ref, kr_ref, vm_ref, *refs):
    osb_ref, omla_ref = refs[n_cast], refs[n_cast + 1]
    _cast_slabs(refs[:n_cast], refs[n_cast + 2:])
    n_tiles = qs_ref.shape[0] // ATTN_TILE
    sb = _sb_pieces(qs_ref, ks_ref, vs_ref, osb_ref)
    mla = _mla_pieces(qm_ref, kn_ref, kr_ref, vm_ref, omla_ref)
    mla_per_tile = 3
    for qi in range(n_tiles):
        order = sorted([(k / (qi + 1), 0) for k in range(qi + 1)] +
                       [((k + 0.5) / mla_per_tile, 1) for k in range(mla_per_tile)])
        for _, stream in order:
            next(mla if stream else sb)
    for _ in sb:
        pass
    for _ in mla:
        pass


def _attention(qs, ks, vs, qm, kn, kr, vm, weights, col_blocks):
    b, s, _ = qs.shape
    head = lambda w: pl.BlockSpec((None, s, w), lambda bi, h: (bi, 0, h))
    slab_in, slab_out, slab_shapes = _slab_specs(weights, col_blocks, b, SB_HEADS)
    return pl.pallas_call(
        functools.partial(_attn_kernel, len(weights)),
        grid=(b, SB_HEADS),
        in_specs=[head(SB_HEAD_DIM)] * 3 +
                 [head(MLA_QK_PAD), head(MLA_NOPE_DIM), pl.BlockSpec((None, s, LANES), lambda bi, h: (bi, 0, 0)),
                  head(MLA_V_DIM)] + slab_in,
        out_specs=[head(SB_HEAD_DIM), head(MLA_V_DIM)] + slab_out,
        out_shape=[jax.ShapeDtypeStruct((b, s, SB_WIDTH), F32), jax.ShapeDtypeStruct((b, s, MLA_WIDTH), F32)] +
                  slab_shapes,
        compiler_params=pltpu.CompilerParams(dimension_semantics=("arbitrary",) * 2, vmem_limit_bytes=VMEM_LIMIT),
        name="attn",
    )(qs, ks, vs, qm, kn, kr, vm, *weights)


def _out_kernel(osb_ref, omla_ref, x_ref, g_sb_ref, g_mla_ref, w_o_ref, g_post_ref, g_ffn_ref, x1_ref, h2_ref):
    rows = osb_ref.shape[0]
    chunks = [slice(r, r + OUT_CHUNK) for r in range(0, rows, OUT_CHUNK)]

    def product(rs):
        n_sb = _rmsnorm(osb_ref[rs, :], g_sb_ref[...]).astype(BF16)
        n_mla = _rmsnorm(omla_ref[rs, :], g_mla_ref[...]).astype(BF16)
        return _dot(jnp.concatenate([n_sb, n_mla], axis=1), w_o_ref[...])

    def finish(rs, y):
        x1 = x_ref[rs, :] + _rmsnorm(y, g_post_ref[...])
        x1_ref[rs, :] = x1
        h2_ref[rs, :] = _rmsnorm(x1, g_ffn_ref[...]).astype(BF16)

    pending = None
    for rs in chunks:
        y = product(rs)
        if pending is not None:
            finish(*pending)
        pending = (rs, y)
    finish(*pending)


def _out_proj(o_sb, o_mla, x2, g_sb, g_mla, w_o, g_post, g_ffn):
    n = x2.shape[0]
    tm = OUT_ROWS
    row = lambda w: pl.BlockSpec((tm, w), lambda i: (i, 0))
    full = _resident
    return pl.pallas_call(
        _out_kernel,
        grid=(n // tm,),
        in_specs=[row(SB_WIDTH), row(MLA_WIDTH), row(D_MODEL), full(g_sb), full(g_mla), full(w_o), full(g_post),
                  full(g_ffn)],
        out_specs=[row(D_MODEL), row(D_MODEL)],
        out_shape=[jax.ShapeDtypeStruct((n, D_MODEL), F32), jax.ShapeDtypeStruct((n, D_MODEL), BF16)],
        compiler_params=pltpu.CompilerParams(dimension_semantics=("arbitrary",), vmem_limit_bytes=VMEM_LIMIT),
        name="out_proj",
    )(o_sb, o_mla, x2, g_sb, g_mla, w_o, g_post, g_ffn)


def _ffn_kernel(seq_tiles, h_ref, x1_ref, wg_ref, wv_ref, cwg_ref, cwv_ref, cbg_ref, cbv_ref, wd_ref,
                g_post_ref, o_ref, pg_ref, pv_ref, tail_ref, act_ref, acc_ref):
    tm = FFN_ROWS
    i = pl.program_id(0)
    j = pl.program_id(1)
    last = pl.num_programs(1) - 1
    first = (i % seq_tiles) == 0

    def conv(p_ref, which, w_ref, cw_ref, cb_ref):
        prev = tail_ref[j, which]
        p_ref[0:SUBLANES, :] = jnp.where(first, jnp.zeros_like(prev), prev)
        p_ref[SUBLANES:, :] = _dot(h_ref[...], w_ref[...])
        tail_ref[j, which] = p_ref[tm:tm + SUBLANES, :]
        u = cb_ref[...] + cw_ref[CONV_WIDTH - 1:CONV_WIDTH, :] * p_ref[SUBLANES:SUBLANES + tm, :]
        for tap in range(1, CONV_WIDTH):
            k = CONV_WIDTH - 1 - tap
            u = u + cw_ref[k:k + 1, :] * p_ref[SUBLANES - tap:SUBLANES - tap + tm, :]
        return u

    def build(slot):
        gate = conv(pg_ref, 0, wg_ref, cwg_ref, cbg_ref)
        val = conv(pv_ref, 1, wv_ref, cwv_ref, cbv_ref)
        act_ref[slot] = (jax.nn.gelu(gate, approximate=True) * val).astype(BF16)

    def multiply(slot):
        return _dot(act_ref[slot], wd_ref[...])

    @pl.when(jnp.logical_and(i == 0, j == 0))
    def _():
        tail_ref[...] = jnp.zeros_like(tail_ref)

    @pl.when(j == 0)
    def _():
        acc_ref[...] = jnp.zeros_like(acc_ref)
        build(0)

    @pl.when(jnp.logical_and(j > 0, j < last))
    def _():
        slot = j % 2
        acc_ref[...] += multiply(1 - slot)
        build(slot)

    @pl.when(j == last)
    def _():
        y = acc_ref[...] + multiply((last - 1) % 2)
        o_ref[...] = x1_ref[...] + _rmsnorm(y, g_post_ref[...])


def _ffn(h2, x1, w_up_blocks, conv_w, conv_b, w_down, g_post, seq_len):
    n = h2.shape[0]
    tm, tn = FFN_ROWS, FFN_COLS
    nj = D_FF // tn
    up_t = lambda j: jnp.minimum(j, nj - 1)
    return pl.pallas_call(
        functools.partial(_ffn_kernel, seq_len // tm),
        grid=(n // tm, nj + 1),
        in_specs=[pl.BlockSpec((tm, D_MODEL), lambda i, j: (i, 0)),
                  pl.BlockSpec((tm, D_MODEL), lambda i, j: (i, 0)),
                  pl.BlockSpec((None, D_MODEL, tn), lambda i, j: (up_t(j), 0, 0)),
                  pl.BlockSpec((None, D_MODEL, tn), lambda i, j: (up_t(j) + nj, 0, 0)),
                  pl.BlockSpec((CONV_WIDTH, tn), lambda i, j: (0, up_t(j))),
                  pl.BlockSpec((CONV_WIDTH, tn), lambda i, j: (0, up_t(j) + nj)),
                  pl.BlockSpec((1, tn), lambda i, j: (0, up_t(j))),
                  pl.BlockSpec((1, tn), lambda i, j: (0, up_t(j) + nj)),
                  pl.BlockSpec((tn, D_MODEL), lambda i, j: (jnp.maximum(j - 1, 0), 0)),
                  pl.BlockSpec((1, D_MODEL), lambda i, j: (0, 0))],
        out_specs=pl.BlockSpec((tm, D_MODEL), lambda i, j: (i, 0)),
        out_shape=jax.ShapeDtypeStruct((n, D_MODEL), F32),
        scratch_shapes=[pltpu.VMEM((tm + SUBLANES, tn), F32), pltpu.VMEM((tm + SUBLANES, tn), F32),
                        pltpu.VMEM((nj, 2, SUBLANES, tn), F32), pltpu.VMEM((2, tm, tn), BF16),
                        pltpu.VMEM((tm, D_MODEL), F32)],
        compiler_params=pltpu.CompilerParams(dimension_semantics=("arbitrary", "arbitrary"),
                                             vmem_limit_bytes=VMEM_LIMIT),
        name="ffn",
    )(h2, x1, w_up_blocks, w_up_blocks, conv_w, conv_w, conv_b, conv_b, w_down, g_post)


def _swap_halves(w):
    half = w.shape[-1] // 2
    return jnp.concatenate([w[..., half:], w[..., :half]], axis=-1)


def _layout_w_kr(w_in):
    k_rope = w_in[:, _C_KR:_C_KR + MLA_ROPE_DIM]
    return jnp.concatenate([k_rope, _swap_halves(k_rope)], axis=1).astype(BF16)


def _layout_w_uq(w_uq):
    w = w_uq.reshape(MLA_Q_RANK, MLA_HEADS, MLA_QK_DIM)
    rope = w[:, :, MLA_NOPE_DIM:]
    w = jnp.concatenate([w[:, :, :MLA_NOPE_DIM], rope, _swap_halves(rope)], axis=-1)
    return w.reshape(MLA_Q_RANK, MLA_HEADS * MLA_QK_PAD).astype(BF16)


def _layout_w_ukv(w_ukv):
    w = w_ukv.reshape(MLA_KV_RANK, MLA_HEADS, MLA_NOPE_DIM + MLA_V_DIM)
    k_nope = w[:, :, :MLA_NOPE_DIM].reshape(MLA_KV_RANK, MLA_WIDTH)
    v = w[:, :, MLA_NOPE_DIM:].reshape(MLA_KV_RANK, MLA_WIDTH)
    return jnp.concatenate([k_nope, v], axis=1).astype(BF16)


def kernel(x, positions, g_attn_pre, w_in, g_cq, w_uq, g_ckv, w_ukv, g_out_sb, g_out_mla, w_o, g_attn_post,
           g_ffn_pre, w_up, conv_w, conv_b, w_down, g_ffn_post):
    b, s, d = x.shape
    depth = w_in.shape[0]
    n = b * s
    half = MLA_ROPE_DIM // 2
    inv_freq = ROPE_THETA ** (-jnp.arange(half, dtype=F32) / half)
    invf = jnp.tile(inv_freq, LANES // half)[None, :]
    pos2 = positions.reshape(n, 1)
    x2 = x.reshape(n, d)
    r2 = lambda g: g[None, :]
    for l in range(depth):
        qsb, ksb, vsb, qm, kn, kr, vm = _proj(
            x2, pos2, invf, r2(g_attn_pre[l]), w_in[l].astype(BF16), _layout_w_kr(w_in[l]), r2(g_cq[l]),
            _layout_w_uq(w_uq[l]), r2(g_ckv[l]), _layout_w_ukv(w_ukv[l]))
        b3 = lambda a: a.reshape(b, s, a.shape[-1])
        o_sb, o_mla, w_up_b, w_o_b, w_down_b = _attention(
            b3(qsb), b3(ksb), b3(vsb), b3(qm), b3(kn), b3(kr), b3(vm), [w_up[l], w_o[l], w_down[l]],
            [FFN_COLS, None, None])
        x1, h2 = _out_proj(o_sb.reshape(n, SB_WIDTH), o_mla.reshape(n, MLA_WIDTH), x2, r2(g_out_sb[l]),
                           r2(g_out_mla[l]), w_o_b, r2(g_attn_post[l]), r2(g_ffn_pre[l]))
        x2 = _ffn(h2, x1, w_up_b, conv_w[l], r2(conv_b[l]), w_down_b, r2(g_ffn_post[l]), s)
    return x2.reshape(b, s, d)
```

```python
import functools

import jax
import jax.numpy as jnp
from jax import lax
from jax.experimental import pallas as pl
from jax.experimental.pallas import tpu as pltpu

F32 = jnp.float32
BF16 = jnp.bfloat16

D_MODEL = 2048
CHUNK = 64
SB_HEADS = 8
SB_HEAD_DIM = 128
SB_WIDTH = SB_HEADS * SB_HEAD_DIM
MLA_HEADS = 8
MLA_NOPE_DIM = 128
MLA_ROPE_DIM = 64
MLA_V_DIM = 128
MLA_Q_RANK = 512
MLA_KV_RANK = 256
MLA_QK_DIM = MLA_NOPE_DIM + MLA_ROPE_DIM
MLA_WIDTH = MLA_HEADS * MLA_V_DIM
D_FF = 5632
CONV_WIDTH = 3
ROPE_THETA = 10000.0
EPS = 1e-6
LOG2_E = 1.4426950408889634

LANES = 128
SUBLANES = 8
MLA_QK_PAD = 2 * LANES

_C_QSB = 0
_C_KSB = SB_WIDTH
_C_VSB = 2 * SB_WIDTH
_C_CQ = 3 * SB_WIDTH
_C_CKV = _C_CQ + MLA_Q_RANK
_C_KR = _C_CKV + MLA_KV_RANK

PROJ_ROWS = 512
PROJ_CHUNK = 256
ATTN_TILE = 256
OUT_ROWS = 512
OUT_CHUNK = 256
FFN_ROWS = 512
FFN_COLS = 512

VMEM_LIMIT = 56 * 1024 * 1024


def _rmsnorm(x, g):
    return x * lax.rsqrt(jnp.mean(x * x, axis=-1, keepdims=True) + EPS) * g


def _resident(a):
    return pl.BlockSpec(a.shape, lambda *_: (0,) * a.ndim, pipeline_mode=pl.Buffered(1))


def _dot(a, b):
    return jnp.dot(a, b, preferred_element_type=F32)


def _dot_nt(a, b):
    return lax.dot_general(a, b, (((1,), (1,)), ((), ())), preferred_element_type=F32)


def _proj_kernel(x_ref, pos_ref, invf_ref, g_pre_ref, w_in_ref, w_kr_ref, g_cq_ref, w_uq_ref, g_ckv_ref, w_ukv_ref,
                 qsb_ref, ksb_ref, vsb_ref, qm_ref, kn_ref, kr_ref, vm_ref):
    sb_scale = SB_HEAD_DIM ** -0.5 * LOG2_E
    mla_scale = MLA_QK_DIM ** -0.5 * LOG2_E

    def wide(rs):
        h = _rmsnorm(x_ref[rs, :], g_pre_ref[...]).astype(BF16)

        def proj(lo, hi):
            return _dot(h, w_in_ref[:, lo:hi])

        cq = proj(_C_CQ, _C_CKV)
        ckv = proj(_C_CKV, _C_KR)
        kr = _dot(h, w_kr_ref[...])
        qsb_ref[rs, :] = (proj(_C_QSB, _C_KSB) * sb_scale).astype(BF16)
        ksb_ref[rs, :] = proj(_C_KSB, _C_VSB).astype(BF16)
        vsb_ref[rs, :] = proj(_C_VSB, _C_CQ).astype(BF16)
        return cq, ckv, kr

    def latent(rs, cq, ckv, kr):
        freqs = MLA_ROPE_DIM // 2
        groups = LANES // freqs
        part = (rs.stop - rs.start) // groups
        pos = pos_ref[rs, :].astype(F32)
        lane = lax.broadcasted_iota(jnp.int32, (part, LANES), 1)
        pos_by_group = pos[0:part]
        for g in range(1, groups):
            pos_by_group = jnp.where(lane >= g * freqs, pos[g * part:(g + 1) * part], pos_by_group)
        ang = pos_by_group * invf_ref[...]

        def spread(table):
            parts = []
            for g in range(groups):
                own = table if g == 0 else pltpu.roll(table, LANES - g * freqs, axis=1)
                twice = jnp.where(lane < freqs, own, pltpu.roll(own, freqs, axis=1))
                parts.append(jnp.where(lane < MLA_ROPE_DIM, twice, 0.0))
            return jnp.concatenate(parts, axis=0)

        t1 = spread(jnp.cos(ang))
        sin = spread(jnp.sin(ang))
        t2 = jnp.where(lax.broadcasted_iota(jnp.int32, sin.shape, 1) < freqs, -sin, sin)

        def rope(g):
            return g * t1 + pltpu.roll(g, MLA_ROPE_DIM, axis=1) * t2

        kr_ref[rs, :] = rope(kr).astype(BF16)

        cq = _rmsnorm(cq, g_cq_ref[...]).astype(BF16)
        for hd in range(MLA_HEADS):
            c0 = hd * MLA_QK_PAD
            qh = _dot(cq, w_uq_ref[:, c0:c0 + MLA_QK_PAD])
            qm_ref[rs, c0:c0 + LANES] = (qh[:, :LANES] * mla_scale).astype(BF16)
            qm_ref[rs, c0 + LANES:c0 + MLA_QK_PAD] = (rope(qh[:, LANES:]) * mla_scale).astype(BF16)

        ckv = _rmsnorm(ckv, g_ckv_ref[...]).astype(BF16)
        kn_ref[rs, :] = _dot(ckv, w_ukv_ref[:, :MLA_WIDTH]).astype(BF16)
        vm_ref[rs, :] = _dot(ckv, w_ukv_ref[:, MLA_WIDTH:]).astype(BF16)

    pending = None
    for r in range(0, x_ref.shape[0], PROJ_CHUNK):
        rs = slice(r, r + PROJ_CHUNK)
        latents = wide(rs)
        if pending is not None:
            latent(*pending)
        pending = (rs,) + latents
    latent(*pending)


def _proj(x2, pos2, invf, g_pre, w_in_b, w_kr_p, g_cq, w_uq_p, g_ckv, w_ukv_p):
    n = x2.shape[0]
    tm = PROJ_ROWS
    row = lambda w: pl.BlockSpec((tm, w), lambda i: (i, 0))
    full = _resident
    out_widths = (SB_WIDTH, SB_WIDTH, SB_WIDTH, MLA_HEADS * MLA_QK_PAD, MLA_WIDTH, LANES, MLA_WIDTH)
    return pl.pallas_call(
        _proj_kernel,
        grid=(n // tm,),
        in_specs=[row(D_MODEL), row(1), full(invf), full(g_pre), full(w_in_b), full(w_kr_p), full(g_cq),
                  full(w_uq_p), full(g_ckv), full(w_ukv_p)],
        out_specs=[row(w) for w in out_widths],
        out_shape=[jax.ShapeDtypeStruct((n, w), BF16) for w in out_widths],
        compiler_params=pltpu.CompilerParams(dimension_semantics=("arbitrary",), vmem_limit_bytes=VMEM_LIMIT),
        name="proj",
    )(x2, pos2, invf, g_pre, w_in_b, w_kr_p, g_cq, w_uq_p, g_ckv, w_ukv_p)


def _softplus2(z):
    return jnp.maximum(z, 0.0) + jnp.log2(1.0 + jnp.exp2(-jnp.abs(z)))


def _cast_slabs(src_refs, dst_refs):
    for src_ref, dst_ref in zip(src_refs, dst_refs):
        if len(dst_ref.shape) == 3:
            width = dst_ref.shape[2]
            for c in range(dst_ref.shape[0]):
                dst_ref[c] = src_ref[:, c * width:(c + 1) * width].astype(BF16)
        else:
            dst_ref[...] = src_ref[...].astype(BF16)


def _slab_specs(weights, col_blocks, batch, heads):
    in_specs, out_specs, out_shapes = [], [], []
    for w, width in zip(weights, col_blocks):
        rows = w.shape[0] // (batch * heads)
        in_specs.append(pl.BlockSpec((rows, w.shape[1]), lambda bi, h: (bi * heads + h, 0)))
        if width is None:
            out_specs.append(pl.BlockSpec((rows, w.shape[1]), lambda bi, h: (bi * heads + h, 0)))
            out_shapes.append(jax.ShapeDtypeStruct(w.shape, BF16))
        else:
            n_blocks = w.shape[1] // width
            out_specs.append(pl.BlockSpec((n_blocks, rows, width), lambda bi, h: (0, bi * heads + h, 0)))
            out_shapes.append(jax.ShapeDtypeStruct((n_blocks, w.shape[0], width), BF16))
    return in_specs, out_specs, out_shapes


def _sb_pieces(q_ref, k_ref, v_ref, o_ref):
    t = ATTN_TILE
    row = lax.broadcasted_iota(jnp.int32, (t, t), 0)
    col = lax.broadcasted_iota(jnp.int32, (t, t), 1)
    mask = col < row
    tri = (row >= col).astype(BF16)
    tri2 = jnp.concatenate([tri, tri], axis=0)

    n_tiles = q_ref.shape[0] // t

    def logits(qi):
        return _dot_nt(q_ref[qi * t:(qi + 1) * t, :], k_ref[0:(qi + 1) * t, :])

    def suffix_sums(zj, diag):
        sp = _softplus2(zj)
        if diag:
            sp = jnp.where(mask, sp, 0.0)
        hi = sp.astype(BF16)
        lo = (sp - hi.astype(F32)).astype(BF16)
        return _dot(jnp.concatenate([hi, lo], axis=1), tri2)

    z = [logits(0)] + [None] * (n_tiles - 1)
    a_tiles = [[None] * (qi + 1) for qi in range(n_tiles)]
    carry = [None] * n_tiles

    def finish(qi, j, zj, local):
        cum = local if carry[qi] is None else local + carry[qi]
        a = jnp.exp2(zj - cum)
        if j == qi:
            a = jnp.where(mask, a, 0.0)
        a_tiles[qi][j] = a.astype(BF16)
        carry[qi] = local[:, 0:1] if carry[qi] is None else carry[qi] + local[:, 0:1]
        if j == 0:
            a_all = a_tiles[qi][0] if qi == 0 else jnp.concatenate(a_tiles[qi], axis=1)
            o_ref[qi * t:(qi + 1) * t, :] = _dot(a_all, v_ref[0:(qi + 1) * t, :])

    pending = None
    for qi in range(n_tiles):
        for j in range(qi, -1, -1):
            if j == qi and qi + 1 < n_tiles:
                z[qi + 1] = logits(qi + 1)
            zj = z[qi][:, j * t:(j + 1) * t]
            local = suffix_sums(zj, j == qi)
            if pending is not None:
                finish(*pending)
            pending = (qi, j, zj, local)
            yield
    finish(*pending)
    yield


def _mla_pieces(q_ref, kn_ref, kr_ref, v_ref, o_ref):
    t = ATTN_TILE
    row = lax.broadcasted_iota(jnp.int32, (t, t), 0)
    col = lax.broadcasted_iota(jnp.int32, (t, t), 1)
    mask = (col // CHUNK) <= (row // CHUNK)

    def scores(qi):
        n = qi + 1
        kcat = jnp.concatenate([kn_ref[0:n * t, :], kr_ref[0:n * t, :]], axis=1)
        return _dot_nt(q_ref[qi * t:n * t, :], kcat)

    n_tiles = q_ref.shape[0] // t
    sc_next = scores(0)
    for qi in range(n_tiles):
        n = qi + 1
        sc = sc_next
        if n < n_tiles:
            sc_next = scores(n)
        yield
        diag = jnp.where(mask, sc[:, qi * t:], -jnp.inf)
        sc = diag if n == 1 else jnp.concatenate([sc[:, :qi * t], diag], axis=1)
        p = jnp.exp2(sc - jnp.max(sc, axis=-1, keepdims=True))
        yield
        vcat = jnp.concatenate([v_ref[0:n * t, :], jnp.ones((n * t, LANES), BF16)], axis=1)
        acc = _dot(p.astype(BF16), vcat)
        o_ref[qi * t:n * t, :] = acc[:, :MLA_V_DIM] / acc[:, MLA_V_DIM:]
        yield


def _attn_kernel(n_cast, qs_ref, ks_ref, vs_ref, qm_ref, kn_ref, kr_ref, vm_ref, *refs):
    osb_ref, omla_ref = refs[n_cast], refs[n_cast + 1]
    _cast_slabs(refs[:n_cast], refs[n_cast + 2:])
    n_tiles = qs_ref.shape[0] // ATTN_TILE
    sb = _sb_pieces(qs_ref, ks_ref, vs_ref, osb_ref)
    mla = _mla_pieces(qm_ref, kn_ref, kr_ref, vm_ref, omla_ref)
    mla_per_tile = 3
    for qi in range(n_tiles):
        order = sorted([(k / (qi + 1), 0) for k in range(qi + 1)] +
                       [((k + 0.5) / mla_per_tile, 1) for k in range(mla_per_tile)])
        for _, stream in order:
            next(mla if stream else sb)
    for _ in sb:
        pass
    for _ in mla:
        pass


def _attention(qs, ks, vs, qm, kn, kr, vm, weights, col_blocks):
    b, s, _ = qs.shape
    head = lambda w: pl.BlockSpec((None, s, w), lambda bi, h: (bi, 0, h))
    slab_in, slab_out, slab_shapes = _slab_specs(weights, col_blocks, b, SB_HEADS)
    return pl.pallas_call(
        functools.partial(_attn_kernel, len(weights)),
        grid=(b, SB_HEADS),
        in_specs=[head(SB_HEAD_DIM)] * 3 +
                 [head(MLA_QK_PAD), head(MLA_NOPE_DIM), pl.BlockSpec((None, s, LANES), lambda bi, h: (bi, 0, 0)),
                  head(MLA_V_DIM)] + slab_in,
        out_specs=[head(SB_HEAD_DIM), head(MLA_V_DIM)] + slab_out,
        out_shape=[jax.ShapeDtypeStruct((b, s, SB_WIDTH), F32), jax.ShapeDtypeStruct((b, s, MLA_WIDTH), F32)] +
                  slab_shapes,
        compiler_params=pltpu.CompilerParams(dimension_semantics=("arbitrary",) * 2, vmem_limit_bytes=VMEM_LIMIT),
        name="attn",
    )(qs, ks, vs, qm, kn, kr, vm, *weights)


def _out_kernel(osb_ref, omla_ref, x_ref, g_sb_ref, g_mla_ref, w_o_ref, g_post_ref, g_ffn_ref, x1_ref, h2_ref):
    rows = osb_ref.shape[0]
    chunks = [slice(r, r + OUT_CHUNK) for r in range(0, rows, OUT_CHUNK)]

    def product(rs):
        n_sb = _rmsnorm(osb_ref[rs, :], g_sb_ref[...]).astype(BF16)
        n_mla = _rmsnorm(omla_ref[rs, :], g_mla_ref[...]).astype(BF16)
        return _dot(jnp.concatenate([n_sb, n_mla], axis=1), w_o_ref[...])

    def finish(rs, y):
        x1 = x_ref[rs, :] + _rmsnorm(y, g_post_ref[...])
        x1_ref[rs, :] = x1
        h2_ref[rs, :] = _rmsnorm(x1, g_ffn_ref[...]).astype(BF16)

    pending = None
    for rs in chunks:
        y = product(rs)
        if pending is not None:
            finish(*pending)
        pending = (rs, y)
    finish(*pending)


def _out_proj(o_sb, o_mla, x2, g_sb, g_mla, w_o, g_post, g_ffn):
    n = x2.shape[0]
    tm = OUT_ROWS
    row = lambda w: pl.BlockSpec((tm, w), lambda i: (i, 0))
    full = _resident
    return pl.pallas_call(
        _out_kernel,
        grid=(n // tm,),
        in_specs=[row(SB_WIDTH), row(MLA_WIDTH), row(D_MODEL), full(g_sb), full(g_mla), full(w_o), full(g_post),
                  full(g_ffn)],
        out_specs=[row(D_MODEL), row(D_MODEL)],
        out_shape=[jax.ShapeDtypeStruct((n, D_MODEL), F32), jax.ShapeDtypeStruct((n, D_MODEL), BF16)],
        compiler_params=pltpu.CompilerParams(dimension_semantics=("arbitrary",), vmem_limit_bytes=VMEM_LIMIT),
        name="out_proj",
    )(o_sb, o_mla, x2, g_sb, g_mla, w_o, g_post, g_ffn)


def _ffn_kernel(seq_tiles, h_ref, x1_ref, wg_ref, wv_ref, cwg_ref, cwv_ref, cbg_ref, cbv_ref, wd_ref,
                g_post_ref, o_ref, pg_ref, pv_ref, tail_ref, act_ref, acc_ref):
    tm = FFN_ROWS
    i = pl.program_id(0)
    j = pl.program_id(1)
    last = pl.num_programs(1) - 1
    first = (i % seq_tiles) == 0

    def project(p_ref, which, w_ref):
        prev = tail_ref[j, which]
        p_ref[0:SUBLANES, :] = jnp.where(first, jnp.zeros_like(prev), prev)
        p_ref[SUBLANES:, :] = _dot(h_ref[...], w_ref[...])
        tail_ref[j, which] = p_ref[tm:tm + SUBLANES, :]

    def conv(p_ref, cw_ref, cb_ref):
        u = cb_ref[...] + cw_ref[CONV_WIDTH - 1:CONV_WIDTH, :] * p_ref[SUBLANES:SUBLANES + tm, :]
        for tap in range(1, CONV_WIDTH):
            k = CONV_WIDTH - 1 - tap
            u = u + cw_ref[k:k + 1, :] * p_ref[SUBLANES - tap:SUBLANES - tap + tm, :]
        return u

    def build(slot, fill=(lambda: None, lambda: None)):
        project(pg_ref, 0, wg_ref)
        fill[0]()
        gate = jax.nn.gelu(conv(pg_ref, cwg_ref, cbg_ref), approximate=True)
        project(pv_ref, 1, wv_ref)
        fill[1]()
        act_ref[slot] = (gate * conv(pv_ref, cwv_ref, cbv_ref)).astype(BF16)

    def multiply(slot, cs=slice(None)):
        return _dot(act_ref[slot], wd_ref[:, cs])

    @pl.when(jnp.logical_and(i == 0, j == 0))
    def _():
        tail_ref[...] = jnp.zeros_like(tail_ref)

    @pl.when(j == 0)
    def _():
        acc_ref[...] = jnp.zeros_like(acc_ref)
        build(0)

    @pl.when(jnp.logical_and(j > 0, j < last))
    def _():
        slot = j % 2
        halves = (slice(0, D_MODEL // 2), slice(D_MODEL // 2, D_MODEL))

        def accumulate(cs):
            def run():
                acc_ref[:, cs] += multiply(1 - slot, cs)
            return run

        build(slot, [accumulate(cs) for cs in halves])

    @pl.when(j == last)
    def _():
        y = acc_ref[...] + multiply((last - 1) % 2)
        o_ref[...] = x1_ref[...] + _rmsnorm(y, g_post_ref[...])


def _ffn(h2, x1, w_up_blocks, conv_w, conv_b, w_down, g_post, seq_len):
    n = h2.shape[0]
    tm, tn = FFN_ROWS, FFN_COLS
    nj = D_FF // tn
    up_t = lambda j: jnp.minimum(j, nj - 1)
    return pl.pallas_call(
        functools.partial(_ffn_kernel, seq_len // tm),
        grid=(n // tm, nj + 1),
        in_specs=[pl.BlockSpec((tm, D_MODEL), lambda i, j: (i, 0)),
                  pl.BlockSpec((tm, D_MODEL), lambda i, j: (i, 0)),
                  pl.BlockSpec((None, D_MODEL, tn), lambda i, j: (up_t(j), 0, 0)),
                  pl.BlockSpec((None, D_MODEL, tn), lambda i, j: (up_t(j) + nj, 0, 0)),
                  pl.BlockSpec((CONV_WIDTH, tn), lambda i, j: (0, up_t(j))),
                  pl.BlockSpec((CONV_WIDTH, tn), lambda i, j: (0, up_t(j) + nj)),
                  pl.BlockSpec((1, tn), lambda i, j: (0, up_t(j))),
                  pl.BlockSpec((1, tn), lambda i, j: (0, up_t(j) + nj)),
                  pl.BlockSpec((tn, D_MODEL), lambda i, j: (jnp.maximum(j - 1, 0), 0)),
                  pl.BlockSpec((1, D_MODEL), lambda i, j: (0, 0))],
        out_specs=pl.BlockSpec((tm, D_MODEL), lambda i, j: (i, 0)),
        out_shape=jax.ShapeDtypeStruct((n, D_MODEL), F32),
        scratch_shapes=[pltpu.VMEM((tm + SUBLANES, tn), F32), pltpu.VMEM((tm + SUBLANES, tn), F32),
                        pltpu.VMEM((nj, 2, SUBLANES, tn), F32), pltpu.VMEM((2, tm, tn), BF16),
                        pltpu.VMEM((tm, D_MODEL), F32)],
        compiler_params=pltpu.CompilerParams(dimension_semantics=("arbitrary", "arbitrary"),
                                             vmem_limit_bytes=VMEM_LIMIT),
        name="ffn",
    )(h2, x1, w_up_blocks, w_up_blocks, conv_w, conv_w, conv_b, conv_b, w_down, g_post)


def _swap_halves(w):
    half = w.shape[-1] // 2
    return jnp.concatenate([w[..., half:], w[..., :half]], axis=-1)


def _layout_w_kr(w_in):
    k_rope = w_in[:, _C_KR:_C_KR + MLA_ROPE_DIM]
    return jnp.concatenate([k_rope, _swap_halves(k_rope)], axis=1).astype(BF16)


def _layout_w_uq(w_uq):
    w = w_uq.reshape(MLA_Q_RANK, MLA_HEADS, MLA_QK_DIM)
    rope = w[:, :, MLA_NOPE_DIM:]
    w = jnp.concatenate([w[:, :, :MLA_NOPE_DIM], rope, _swap_halves(rope)], axis=-1)
    return w.reshape(MLA_Q_RANK, MLA_HEADS * MLA_QK_PAD).astype(BF16)


def _layout_w_ukv(w_ukv):
    w = w_ukv.reshape(MLA_KV_RANK, MLA_HEADS, MLA_NOPE_DIM + MLA_V_DIM)
    k_nope = w[:, :, :MLA_NOPE_DIM].reshape(MLA_KV_RANK, MLA_WIDTH)
    v = w[:, :, MLA_NOPE_DIM:].reshape(MLA_KV_RANK, MLA_WIDTH)
    return jnp.concatenate([k_nope, v], axis=1).astype(BF16)


def kernel(x, positions, g_attn_pre, w_in, g_cq, w_uq, g_ckv, w_ukv, g_out_sb, g_out_mla, w_o, g_attn_post,
           g_ffn_pre, w_up, conv_w, conv_b, w_down, g_ffn_post):
    b, s, d = x.shape
    depth = w_in.shape[0]
    n = b * s
    half = MLA_ROPE_DIM // 2
    inv_freq = ROPE_THETA ** (-jnp.arange(half, dtype=F32) / half)
    invf = jnp.tile(inv_freq, LANES // half)[None, :]
    pos2 = positions.reshape(n, 1)
    x2 = x.reshape(n, d)
    r2 = lambda g: g[None, :]
    for l in range(depth):
        qsb, ksb, vsb, qm, kn, kr, vm = _proj(
            x2, pos2, invf, r2(g_attn_pre[l]), w_in[l].astype(BF16), _layout_w_kr(w_in[l]), r2(g_cq[l]),
            _layout_w_uq(w_uq[l]), r2(g_ckv[l]), _layout_w_ukv(w_ukv[l]))
        b3 = lambda a: a.reshape(b, s, a.shape[-1])
        o_sb, o_mla, w_up_b, w_o_b, w_down_b = _attention(
            b3(qsb), b3(ksb), b3(vsb), b3(qm), b3(kn), b3(kr), b3(vm), [w_up[l], w_o[l], w_down[l]],
            [FFN_COLS, None, None])
        x1, h2 = _out_proj(o_sb.reshape(n, SB_WIDTH), o_mla.reshape(n, MLA_WIDTH), x2, r2(g_out_sb[l]),
                           r2(g_out_mla[l]), w_o_b, r2(g_attn_post[l]), r2(g_ffn_pre[l]))
        x2 = _ffn(h2, x1, w_up_b, conv_w[l], r2(conv_b[l]), w_down_b, r2(g_ffn_post[l]), s)
    return x2.reshape(b, s, d)
```

```python
import functools

import jax
import jax.numpy as jnp
from jax import lax
from jax.experimental import pallas as pl
from jax.experimental.pallas import tpu as pltpu

F32 = jnp.float32
BF16 = jnp.bfloat16

D_MODEL = 2048
CHUNK = 64
SB_HEADS = 8
SB_HEAD_DIM = 128
SB_WIDTH = SB_HEADS * SB_HEAD_DIM
MLA_HEADS = 8
MLA_NOPE_DIM = 128
MLA_ROPE_DIM = 64
MLA_V_DIM = 128
MLA_Q_RANK = 512
MLA_KV_RANK = 256
MLA_QK_DIM = MLA_NOPE_DIM + MLA_ROPE_DIM
MLA_WIDTH = MLA_HEADS * MLA_V_DIM
D_FF = 5632
CONV_WIDTH = 3
ROPE_THETA = 10000.0
EPS = 1e-6
LOG2_E = 1.4426950408889634

LANES = 128
SUBLANES = 8
MLA_QK_PAD = 2 * LANES

_C_QSB = 0
_C_KSB = SB_WIDTH
_C_VSB = 2 * SB_WIDTH
_C_CQ = 3 * SB_WIDTH
_C_CKV = _C_CQ + MLA_Q_RANK
_C_KR = _C_CKV + MLA_KV_RANK

PROJ_ROWS = 512
PROJ_CHUNK = 256
ATTN_TILE = 256
OUT_ROWS = 512
OUT_CHUNK = 256
FFN_ROWS = 512
FFN_COLS = 512
FFN_FILL_SPLIT = 768

VMEM_LIMIT = 56 * 1024 * 1024


def _rmsnorm(x, g):
    return x * lax.rsqrt(jnp.mean(x * x, axis=-1, keepdims=True) + EPS) * g


def _resident(a):
    return pl.BlockSpec(a.shape, lambda *_: (0,) * a.ndim, pipeline_mode=pl.Buffered(1))


def _dot(a, b):
    return jnp.dot(a, b, preferred_element_type=F32)


def _dot_nt(a, b):
    return lax.dot_general(a, b, (((1,), (1,)), ((), ())), preferred_element_type=F32)


def _proj_kernel(x_ref, pos_ref, invf_ref, g_pre_ref, w_in_ref, w_kr_ref, g_cq_ref, w_uq_ref, g_ckv_ref, w_ukv_ref,
                 qsb_ref, ksb_ref, vsb_ref, qm_ref, kn_ref, kr_ref, vm_ref):
    sb_scale = SB_HEAD_DIM ** -0.5 * LOG2_E
    mla_scale = MLA_QK_DIM ** -0.5 * LOG2_E

    def wide(rs):
        h = _rmsnorm(x_ref[rs, :], g_pre_ref[...]).astype(BF16)

        def proj(lo, hi):
            return _dot(h, w_in_ref[:, lo:hi])

        cq = proj(_C_CQ, _C_CKV)
        ckv = proj(_C_CKV, _C_KR)
        kr = _dot(h, w_kr_ref[...])
        qsb_ref[rs, :] = (proj(_C_QSB, _C_KSB) * sb_scale).astype(BF16)
        ksb_ref[rs, :] = proj(_C_KSB, _C_VSB).astype(BF16)
        vsb_ref[rs, :] = proj(_C_VSB, _C_CQ).astype(BF16)
        return cq, ckv, kr

    def latent(rs, cq, ckv, kr):
        freqs = MLA_ROPE_DIM // 2
        groups = LANES // freqs
        part = (rs.stop - rs.start) // groups
        pos = pos_ref[rs, :].astype(F32)
        lane = lax.broadcasted_iota(jnp.int32, (part, LANES), 1)
        pos_by_group = pos[0:part]
        for g in range(1, groups):
            pos_by_group = jnp.where(lane >= g * freqs, pos[g * part:(g + 1) * part], pos_by_group)
        ang = pos_by_group * invf_ref[...]

        def spread(table):
            parts = []
            for g in range(groups):
                own = table if g == 0 else pltpu.roll(table, LANES - g * freqs, axis=1)
                twice = jnp.where(lane < freqs, own, pltpu.roll(own, freqs, axis=1))
                parts.append(jnp.where(lane < MLA_ROPE_DIM, twice, 0.0))
            return jnp.concatenate(parts, axis=0)

        t1 = spread(jnp.cos(ang))
        sin = spread(jnp.sin(ang))
        t2 = jnp.where(lax.broadcasted_iota(jnp.int32, sin.shape, 1) < freqs, -sin, sin)

        def rope(g):
            return g * t1 + pltpu.roll(g, MLA_ROPE_DIM, axis=1) * t2

        kr_ref[rs, :] = rope(kr).astype(BF16)

        cq = _rmsnorm(cq, g_cq_ref[...]).astype(BF16)
        for hd in range(MLA_HEADS):
            c0 = hd * MLA_QK_PAD
            qh = _dot(cq, w_uq_ref[:, c0:c0 + MLA_QK_PAD])
            qm_ref[rs, c0:c0 + LANES] = (qh[:, :LANES] * mla_scale).astype(BF16)
            qm_ref[rs, c0 + LANES:c0 + MLA_QK_PAD] = (rope(qh[:, LANES:]) * mla_scale).astype(BF16)

        ckv = _rmsnorm(ckv, g_ckv_ref[...]).astype(BF16)
        kn_ref[rs, :] = _dot(ckv, w_ukv_ref[:, :MLA_WIDTH]).astype(BF16)
        vm_ref[rs, :] = _dot(ckv, w_ukv_ref[:, MLA_WIDTH:]).astype(BF16)

    pending = None
    for r in range(0, x_ref.shape[0], PROJ_CHUNK):
        rs = slice(r, r + PROJ_CHUNK)
        latents = wide(rs)
        if pending is not None:
            latent(*pending)
        pending = (rs,) + latents
    latent(*pending)


def _proj(x2, pos2, invf, g_pre, w_in_b, w_kr_p, g_cq, w_uq_p, g_ckv, w_ukv_p):
    n = x2.shape[0]
    tm = PROJ_ROWS
    row = lambda w: pl.BlockSpec((tm, w), lambda i: (i, 0))
    full = _resident
    out_widths = (SB_WIDTH, SB_WIDTH, SB_WIDTH, MLA_HEADS * MLA_QK_PAD, MLA_WIDTH, LANES, MLA_WIDTH)
    return pl.pallas_call(
        _proj_kernel,
        grid=(n // tm,),
        in_specs=[row(D_MODEL), row(1), full(invf), full(g_pre), full(w_in_b), full(w_kr_p), full(g_cq),
                  full(w_uq_p), full(g_ckv), full(w_ukv_p)],
        out_specs=[row(w) for w in out_widths],
        out_shape=[jax.ShapeDtypeStruct((n, w), BF16) for w in out_widths],
        compiler_params=pltpu.CompilerParams(dimension_semantics=("arbitrary",), vmem_limit_bytes=VMEM_LIMIT),
        name="proj",
    )(x2, pos2, invf, g_pre, w_in_b, w_kr_p, g_cq, w_uq_p, g_ckv, w_ukv_p)


def _softplus2(z):
    return jnp.maximum(z, 0.0) + jnp.log2(1.0 + jnp.exp2(-jnp.abs(z)))


def _cast_slabs(src_refs, dst_refs):
    for src_ref, dst_ref in zip(src_refs, dst_refs):
        if len(dst_ref.shape) == 3:
            width = dst_ref.shape[2]
            for c in range(dst_ref.shape[0]):
                dst_ref[c] = src_ref[:, c * width:(c + 1) * width].astype(BF16)
        else:
            dst_ref[...] = src_ref[...].astype(BF16)


def _slab_specs(weights, col_blocks, batch, heads):
    in_specs, out_specs, out_shapes = [], [], []
    for w, width in zip(weights, col_blocks):
        rows = w.shape[0] // (batch * heads)
        in_specs.append(pl.BlockSpec((rows, w.shape[1]), lambda bi, h: (bi * heads + h, 0)))
        if width is None:
            out_specs.append(pl.BlockSpec((rows, w.shape[1]), lambda bi, h: (bi * heads + h, 0)))
            out_shapes.append(jax.ShapeDtypeStruct(w.shape, BF16))
        else:
            n_blocks = w.shape[1] // width
            out_specs.append(pl.BlockSpec((n_blocks, rows, width), lambda bi, h: (0, bi * heads + h, 0)))
            out_shapes.append(jax.ShapeDtypeStruct((n_blocks, w.shape[0], width), BF16))
    return in_specs, out_specs, out_shapes


def _sb_pieces(q_ref, k_ref, v_ref, o_ref):
    t = ATTN_TILE
    row = lax.broadcasted_iota(jnp.int32, (t, t), 0)
    col = lax.broadcasted_iota(jnp.int32, (t, t), 1)
    mask = col < row
    tri = (row >= col).astype(BF16)
    tri2 = jnp.concatenate([tri, tri], axis=0)

    n_tiles = q_ref.shape[0] // t

    def logits(qi):
        return _dot_nt(q_ref[qi * t:(qi + 1) * t, :], k_ref[0:(qi + 1) * t, :])

    def suffix_sums(zj, diag):
        sp = _softplus2(zj)
        if diag:
            sp = jnp.where(mask, sp, 0.0)
        hi = sp.astype(BF16)
        lo = (sp - hi.astype(F32)).astype(BF16)
        return _dot(jnp.concatenate([hi, lo], axis=1), tri2)

    z = [logits(0)] + [None] * (n_tiles - 1)
    a_tiles = [[None] * (qi + 1) for qi in range(n_tiles)]
    carry = [None] * n_tiles

    def finish(qi, j, zj, local):
        cum = local if carry[qi] is None else local + carry[qi]
        a = jnp.exp2(zj - cum)
        if j == qi:
            a = jnp.where(mask, a, 0.0)
        a_tiles[qi][j] = a.astype(BF16)
        carry[qi] = local[:, 0:1] if carry[qi] is None else carry[qi] + local[:, 0:1]
        if j == 0:
            a_all = a_tiles[qi][0] if qi == 0 else jnp.concatenate(a_tiles[qi], axis=1)
            o_ref[qi * t:(qi + 1) * t, :] = _dot(a_all, v_ref[0:(qi + 1) * t, :])

    pending = None
    for qi in range(n_tiles):
        for j in range(qi, -1, -1):
            if j == qi and qi + 1 < n_tiles:
                z[qi + 1] = logits(qi + 1)
            zj = z[qi][:, j * t:(j + 1) * t]
            local = suffix_sums(zj, j == qi)
            if pending is not None:
                finish(*pending)
            pending = (qi, j, zj, local)
            yield
    finish(*pending)
    yield


def _mla_pieces(q_ref, kn_ref, kr_ref, v_ref, o_ref):
    t = ATTN_TILE
    row = lax.broadcasted_iota(jnp.int32, (t, t), 0)
    col = lax.broadcasted_iota(jnp.int32, (t, t), 1)
    mask = (col // CHUNK) <= (row // CHUNK)

    def scores(qi):
        n = qi + 1
        kcat = jnp.concatenate([kn_ref[0:n * t, :], kr_ref[0:n * t, :]], axis=1)
        return _dot_nt(q_ref[qi * t:n * t, :], kcat)

    n_tiles = q_ref.shape[0] // t
    sc_next = scores(0)
    for qi in range(n_tiles):
        n = qi + 1
        sc = sc_next
        if n < n_tiles:
            sc_next = scores(n)
        yield
        diag = jnp.where(mask, sc[:, qi * t:], -jnp.inf)
        sc = diag if n == 1 else jnp.concatenate([sc[:, :qi * t], diag], axis=1)
        p = jnp.exp2(sc - jnp.max(sc, axis=-1, keepdims=True))
        yield
        vcat = jnp.concatenate([v_ref[0:n * t, :], jnp.ones((n * t, LANES), BF16)], axis=1)
        acc = _dot(p.astype(BF16), vcat)
        o_ref[qi * t:n * t, :] = acc[:, :MLA_V_DIM] / acc[:, MLA_V_DIM:]
        yield


def _attn_kernel(n_cast, qs_ref, ks_ref, vs_ref, qm_ref, kn_ref, kr_ref, vm_ref, *refs):
    osb_ref, omla_ref = refs[n_cast], refs[n_cast + 1]
    _cast_slabs(refs[:n_cast], refs[n_cast + 2:])
    n_tiles = qs_ref.shape[0] // ATTN_TILE
    sb = _sb_pieces(qs_ref, ks_ref, vs_ref, osb_ref)
    mla = _mla_pieces(qm_ref, kn_ref, kr_ref, vm_ref, omla_ref)
    mla_per_tile = 3
    for qi in range(n_tiles):
        order = sorted([(k / (qi + 1), 0) for k in range(qi + 1)] +
                       [((k + 0.5) / mla_per_tile, 1) for k in range(mla_per_tile)])
        for _, stream in order:
            next(mla if stream else sb)
    for _ in sb:
        pass
    for _ in mla:
        pass


def _attention(qs, ks, vs, qm, kn, kr, vm, weights, col_blocks):
    b, s, _ = qs.shape
    head = lambda w: pl.BlockSpec((None, s, w), lambda bi, h: (bi, 0, h))
    slab_in, slab_out, slab_shapes = _slab_specs(weights, col_blocks, b, SB_HEADS)
    return pl.pallas_call(
        functools.partial(_attn_kernel, len(weights)),
        grid=(b, SB_HEADS),
        in_specs=[head(SB_HEAD_DIM)] * 3 +
                 [head(MLA_QK_PAD), head(MLA_NOPE_DIM), pl.BlockSpec((None, s, LANES), lambda bi, h: (bi, 0, 0)),
                  head(MLA_V_DIM)] + slab_in,
        out_specs=[head(SB_HEAD_DIM), head(MLA_V_DIM)] + slab_out,
        out_shape=[jax.ShapeDtypeStruct((b, s, SB_WIDTH), F32), jax.ShapeDtypeStruct((b, s, MLA_WIDTH), F32)] +
                  slab_shapes,
        compiler_params=pltpu.CompilerParams(dimension_semantics=("arbitrary",) * 2, vmem_limit_bytes=VMEM_LIMIT),
        name="attn",
    )(qs, ks, vs, qm, kn, kr, vm, *weights)


def _out_kernel(osb_ref, omla_ref, x_ref, g_sb_ref, g_mla_ref, w_o_ref, g_post_ref, g_ffn_ref, x1_ref, h2_ref):
    rows = osb_ref.shape[0]
    chunks = [slice(r, r + OUT_CHUNK) for r in range(0, rows, OUT_CHUNK)]

    def product(rs):
        n_sb = _rmsnorm(osb_ref[rs, :], g_sb_ref[...]).astype(BF16)
        n_mla = _rmsnorm(omla_ref[rs, :], g_mla_ref[...]).astype(BF16)
        return _dot(jnp.concatenate([n_sb, n_mla], axis=1), w_o_ref[...])

    def finish(rs, y):
        x1 = x_ref[rs, :] + _rmsnorm(y, g_post_ref[...])
        x1_ref[rs, :] = x1
        h2_ref[rs, :] = _rmsnorm(x1, g_ffn_ref[...]).astype(BF16)

    pending = None
    for rs in chunks:
        y = product(rs)
        if pending is not None:
            finish(*pending)
        pending = (rs, y)
    finish(*pending)


def _out_proj(o_sb, o_mla, x2, g_sb, g_mla, w_o, g_post, g_ffn):
    n = x2.shape[0]
    tm = OUT_ROWS
    row = lambda w: pl.BlockSpec((tm, w), lambda i: (i, 0))
    full = _resident
    return pl.pallas_call(
        _out_kernel,
        grid=(n // tm,),
        in_specs=[row(SB_WIDTH), row(MLA_WIDTH), row(D_MODEL), full(g_sb), full(g_mla), full(w_o), full(g_post),
                  full(g_ffn)],
        out_specs=[row(D_MODEL), row(D_MODEL)],
        out_shape=[jax.ShapeDtypeStruct((n, D_MODEL), F32), jax.ShapeDtypeStruct((n, D_MODEL), BF16)],
        compiler_params=pltpu.CompilerParams(dimension_semantics=("arbitrary",), vmem_limit_bytes=VMEM_LIMIT),
        name="out_proj",
    )(o_sb, o_mla, x2, g_sb, g_mla, w_o, g_post, g_ffn)


def _ffn_kernel(seq_tiles, h_ref, x1_ref, wg_ref, wv_ref, cwg_ref, cwv_ref, cbg_ref, cbv_ref, wd_ref,
                g_post_ref, o_ref, pg_ref, pv_ref, tail_ref, act_ref, acc_ref):
    tm = FFN_ROWS
    i = pl.program_id(0)
    j = pl.program_id(1)
    last = pl.num_programs(1) - 1
    first = (i % seq_tiles) == 0

    def project(p_ref, which, w_ref):
        prev = tail_ref[j, which]
        p_ref[0:SUBLANES, :] = jnp.where(first, jnp.zeros_like(prev), prev)
        p_ref[SUBLANES:, :] = _dot(h_ref[...], w_ref[...])
        tail_ref[j, which] = p_ref[tm:tm + SUBLANES, :]

    def conv(p_ref, cw_ref, cb_ref):
        u = cb_ref[...] + cw_ref[CONV_WIDTH - 1:CONV_WIDTH, :] * p_ref[SUBLANES:SUBLANES + tm, :]
        for tap in range(1, CONV_WIDTH):
            k = CONV_WIDTH - 1 - tap
            u = u + cw_ref[k:k + 1, :] * p_ref[SUBLANES - tap:SUBLANES - tap + tm, :]
        return u

    def build(slot, fill=(lambda: None, lambda: None)):
        project(pg_ref, 0, wg_ref)
        fill[0]()
        gate = jax.nn.gelu(conv(pg_ref, cwg_ref, cbg_ref), approximate=True)
        project(pv_ref, 1, wv_ref)
        fill[1]()
        act_ref[slot] = (gate * conv(pv_ref, cwv_ref, cbv_ref)).astype(BF16)

    def multiply(slot, cs=slice(None)):
        return _dot(act_ref[slot], wd_ref[:, cs])

    @pl.when(jnp.logical_and(i == 0, j == 0))
    def _():
        tail_ref[...] = jnp.zeros_like(tail_ref)

    @pl.when(j == 0)
    def _():
        acc_ref[...] = jnp.zeros_like(acc_ref)
        build(0)

    @pl.when(jnp.logical_and(j > 0, j < last))
    def _():
        slot = j % 2
        split = FFN_FILL_SPLIT
        shares = (slice(0, split), slice(split, D_MODEL))

        def accumulate(cs):
            def run():
                acc_ref[:, cs] += multiply(1 - slot, cs)
            return run

        build(slot, [accumulate(cs) for cs in shares])

    @pl.when(j == last)
    def _():
        y = acc_ref[...] + multiply((last - 1) % 2)
        o_ref[...] = x1_ref[...] + _rmsnorm(y, g_post_ref[...])


def _ffn(h2, x1, w_up_blocks, conv_w, conv_b, w_down, g_post, seq_len):
    n = h2.shape[0]
    tm, tn = FFN_ROWS, FFN_COLS
    nj = D_FF // tn
    up_t = lambda j: jnp.minimum(j, nj - 1)
    return pl.pallas_call(
        functools.partial(_ffn_kernel, seq_len // tm),
        grid=(n // tm, nj + 1),
        in_specs=[pl.BlockSpec((tm, D_MODEL), lambda i, j: (i, 0)),
                  pl.BlockSpec((tm, D_MODEL), lambda i, j: (i, 0)),
                  pl.BlockSpec((None, D_MODEL, tn), lambda i, j: (up_t(j), 0, 0)),
                  pl.BlockSpec((None, D_MODEL, tn), lambda i, j: (up_t(j) + nj, 0, 0)),
                  pl.BlockSpec((CONV_WIDTH, tn), lambda i, j: (0, up_t(j))),
                  pl.BlockSpec((CONV_WIDTH, tn), lambda i, j: (0, up_t(j) + nj)),
                  pl.BlockSpec((1, tn), lambda i, j: (0, up_t(j))),
                  pl.BlockSpec((1, tn), lambda i, j: (0, up_t(j) + nj)),
                  pl.BlockSpec((tn, D_MODEL), lambda i, j: (jnp.maximum(j - 1, 0), 0)),
                  pl.BlockSpec((1, D_MODEL), lambda i, j: (0, 0))],
        out_specs=pl.BlockSpec((tm, D_MODEL), lambda i, j: (i, 0)),
        out_shape=jax.ShapeDtypeStruct((n, D_MODEL), F32),
        scratch_shapes=[pltpu.VMEM((tm + SUBLANES, tn), F32), pltpu.VMEM((tm + SUBLANES, tn), F32),
                        pltpu.VMEM((nj, 2, SUBLANES, tn), F32), pltpu.VMEM((2, tm, tn), BF16),
                        pltpu.VMEM((tm, D_MODEL), F32)],
        compiler_params=pltpu.CompilerParams(dimension_semantics=("arbitrary", "arbitrary"),
                                             vmem_limit_bytes=VMEM_LIMIT),
        name="ffn",
    )(h2, x1, w_up_blocks, w_up_blocks, conv_w, conv_w, conv_b, conv_b, w_down, g_post)


def _swap_halves(w):
    half = w.shape[-1] // 2
    return jnp.concatenate([w[..., half:], w[..., :half]], axis=-1)


def _layout_w_kr(w_in):
    k_rope = w_in[:, _C_KR:_C_KR + MLA_ROPE_DIM]
    return jnp.concatenate([k_rope, _swap_halves(k_rope)], axis=1).astype(BF16)


def _layout_w_uq(w_uq):
    w = w_uq.reshape(MLA_Q_RANK, MLA_HEADS, MLA_QK_DIM)
    rope = w[:, :, MLA_NOPE_DIM:]
    w = jnp.concatenate([w[:, :, :MLA_NOPE_DIM], rope, _swap_halves(rope)], axis=-1)
    return w.reshape(MLA_Q_RANK, MLA_HEADS * MLA_QK_PAD).astype(BF16)


def _layout_w_ukv(w_ukv):
    w = w_ukv.reshape(MLA_KV_RANK, MLA_HEADS, MLA_NOPE_DIM + MLA_V_DIM)
    k_nope = w[:, :, :MLA_NOPE_DIM].reshape(MLA_KV_RANK, MLA_WIDTH)
    v = w[:, :, MLA_NOPE_DIM:].reshape(MLA_KV_RANK, MLA_WIDTH)
    return jnp.concatenate([k_nope, v], axis=1).astype(BF16)


def kernel(x, positions, g_attn_pre, w_in, g_cq, w_uq, g_ckv, w_ukv, g_out_sb, g_out_mla, w_o, g_attn_post,
           g_ffn_pre, w_up, conv_w, conv_b, w_down, g_ffn_post):
    b, s, d = x.shape
    depth = w_in.shape[0]
    n = b * s
    half = MLA_ROPE_DIM // 2
    inv_freq = ROPE_THETA ** (-jnp.arange(half, dtype=F32) / half)
    invf = jnp.tile(inv_freq, LANES // half)[None, :]
    pos2 = positions.reshape(n, 1)
    x2 = x.reshape(n, d)
    r2 = lambda g: g[None, :]
    for l in range(depth):
        qsb, ksb, vsb, qm, kn, kr, vm = _proj(
            x2, pos2, invf, r2(g_attn_pre[l]), w_in[l].astype(BF16), _layout_w_kr(w_in[l]), r2(g_cq[l]),
            _layout_w_uq(w_uq[l]), r2(g_ckv[l]), _layout_w_ukv(w_ukv[l]))
        b3 = lambda a: a.reshape(b, s, a.shape[-1])
        o_sb, o_mla, w_up_b, w_o_b, w_down_b = _attention(
            b3(qsb), b3(ksb), b3(vsb), b3(qm), b3(kn), b3(kr), b3(vm), [w_up[l], w_o[l], w_down[l]],
            [FFN_COLS, None, None])
        x1, h2 = _out_proj(o_sb.reshape(n, SB_WIDTH), o_mla.reshape(n, MLA_WIDTH), x2, r2(g_out_sb[l]),
                           r2(g_out_mla[l]), w_o_b, r2(g_attn_post[l]), r2(g_ffn_pre[l]))
        x2 = _ffn(h2, x1, w_up_b, conv_w[l], r2(conv_b[l]), w_down_b, r2(g_ffn_post[l]), s)
    return x2.reshape(b, s, d)
```

```python
import functools

import jax
import jax.numpy as jnp
from jax import lax
from jax.experimental import pallas as pl
from jax.experimental.pallas import tpu as pltpu

F32 = jnp.float32
BF16 = jnp.bfloat16

D_MODEL = 2048
CHUNK = 64
SB_HEADS = 8
SB_HEAD_DIM = 128
SB_WIDTH = SB_HEADS * SB_HEAD_DIM
MLA_HEADS = 8
MLA_NOPE_DIM = 128
MLA_ROPE_DIM = 64
MLA_V_DIM = 128
MLA_Q_RANK = 512
MLA_KV_RANK = 256
MLA_QK_DIM = MLA_NOPE_DIM + MLA_ROPE_DIM
MLA_WIDTH = MLA_HEADS * MLA_V_DIM
D_FF = 5632
CONV_WIDTH = 3
ROPE_THETA = 10000.0
EPS = 1e-6
LOG2_E = 1.4426950408889634

LANES = 128
SUBLANES = 8
MLA_QK_PAD = 2 * LANES

_C_QSB = 0
_C_KSB = SB_WIDTH
_C_VSB = 2 * SB_WIDTH
_C_CQ = 3 * SB_WIDTH
_C_CKV = _C_CQ + MLA_Q_RANK
_C_KR = _C_CKV + MLA_KV_RANK

PROJ_ROWS = 512
PROJ_CHUNK = 256
ATTN_TILE = 256
OUT_ROWS = 512
OUT_CHUNK = 256
FFN_ROWS = 512
FFN_COLS = 512
FFN_FILL_SPLIT = 768

VMEM_LIMIT = 56 * 1024 * 1024


def _rmsnorm(x, g):
    return x * lax.rsqrt(jnp.mean(x * x, axis=-1, keepdims=True) + EPS) * g


def _resident(a):
    return pl.BlockSpec(a.shape, lambda *_: (0,) * a.ndim, pipeline_mode=pl.Buffered(1))


def _dot(a, b):
    return jnp.dot(a, b, preferred_element_type=F32)


def _dot_nt(a, b):
    return lax.dot_general(a, b, (((1,), (1,)), ((), ())), preferred_element_type=F32)


def _proj_kernel(x_ref, pos_ref, invf_ref, g_pre_ref, w_in_ref, w_kr_ref, g_cq_ref, w_uq_ref, g_ckv_ref, w_ukv_ref,
                 qsb_ref, ksb_ref, vsb_ref, qm_ref, kn_ref, kr_ref, vm_ref):
    sb_scale = SB_HEAD_DIM ** -0.5 * LOG2_E
    mla_scale = MLA_QK_DIM ** -0.5 * LOG2_E

    def wide(rs):
        h = _rmsnorm(x_ref[rs, :], g_pre_ref[...]).astype(BF16)

        def proj(lo, hi):
            return _dot(h, w_in_ref[:, lo:hi])

        cq = proj(_C_CQ, _C_CKV)
        ckv = proj(_C_CKV, _C_KR)
        kr = _dot(h, w_kr_ref[...])
        qsb_ref[rs, :] = (proj(_C_QSB, _C_KSB) * sb_scale).astype(BF16)
        ksb_ref[rs, :] = proj(_C_KSB, _C_VSB).astype(BF16)
        vsb_ref[rs, :] = proj(_C_VSB, _C_CQ).astype(BF16)
        return cq, ckv, kr

    def latent(rs, cq, ckv, kr):
        freqs = MLA_ROPE_DIM // 2
        groups = LANES // freqs
        part = (rs.stop - rs.start) // groups
        pos = pos_ref[rs, :].astype(F32)
        lane = lax.broadcasted_iota(jnp.int32, (part, LANES), 1)
        pos_by_group = pos[0:part]
        for g in range(1, groups):
            pos_by_group = jnp.where(lane >= g * freqs, pos[g * part:(g + 1) * part], pos_by_group)
        ang = pos_by_group * invf_ref[...]

        def spread(table):
            parts = []
            for g in range(groups):
                own = table if g == 0 else pltpu.roll(table, LANES - g * freqs, axis=1)
                twice = jnp.where(lane < freqs, own, pltpu.roll(own, freqs, axis=1))
                parts.append(jnp.where(lane < MLA_ROPE_DIM, twice, 0.0))
            return jnp.concatenate(parts, axis=0)

        t1 = spread(jnp.cos(ang))
        sin = spread(jnp.sin(ang))
        t2 = jnp.where(lax.broadcasted_iota(jnp.int32, sin.shape, 1) < freqs, -sin, sin)

        def rope(g):
            return g * t1 + pltpu.roll(g, MLA_ROPE_DIM, axis=1) * t2

        kr_ref[rs, :] = rope(kr).astype(BF16)

        cq = _rmsnorm(cq, g_cq_ref[...]).astype(BF16)
        for hd in range(MLA_HEADS):
            c0 = hd * MLA_QK_PAD
            qh = _dot(cq, w_uq_ref[:, c0:c0 + MLA_QK_PAD])
            qm_ref[rs, c0:c0 + LANES] = (qh[:, :LANES] * mla_scale).astype(BF16)
            qm_ref[rs, c0 + LANES:c0 + MLA_QK_PAD] = (rope(qh[:, LANES:]) * mla_scale).astype(BF16)

        ckv = _rmsnorm(ckv, g_ckv_ref[...]).astype(BF16)
        kn_ref[rs, :] = _dot(ckv, w_ukv_ref[:, :MLA_WIDTH]).astype(BF16)
        vm_ref[rs, :] = _dot(ckv, w_ukv_ref[:, MLA_WIDTH:]).astype(BF16)

    pending = None
    for r in range(0, x_ref.shape[0], PROJ_CHUNK):
        rs = slice(r, r + PROJ_CHUNK)
        latents = wide(rs)
        if pending is not None:
            latent(*pending)
        pending = (rs,) + latents
    latent(*pending)


def _proj(x2, pos2, invf, g_pre, w_in_b, w_kr_p, g_cq, w_uq_p, g_ckv, w_ukv_p):
    n = x2.shape[0]
    tm = PROJ_ROWS
    row = lambda w: pl.BlockSpec((tm, w), lambda i: (i, 0))
    full = _resident
    out_widths = (SB_WIDTH, SB_WIDTH, SB_WIDTH, MLA_HEADS * MLA_QK_PAD, MLA_WIDTH, LANES, MLA_WIDTH)
    return pl.pallas_call(
        _proj_kernel,
        grid=(n // tm,),
        in_specs=[row(D_MODEL), row(1), full(invf), full(g_pre), full(w_in_b), full(w_kr_p), full(g_cq),
                  full(w_uq_p), full(g_ckv), full(w_ukv_p)],
        out_specs=[row(w) for w in out_widths],
        out_shape=[jax.ShapeDtypeStruct((n, w), BF16) for w in out_widths],
        compiler_params=pltpu.CompilerParams(dimension_semantics=("arbitrary",), vmem_limit_bytes=VMEM_LIMIT),
        name="proj",
    )(x2, pos2, invf, g_pre, w_in_b, w_kr_p, g_cq, w_uq_p, g_ckv, w_ukv_p)


def _softplus2(z):
    return jnp.maximum(z, 0.0) + jnp.log2(1.0 + jnp.exp2(-jnp.abs(z)))


def _cast_slabs(src_refs, dst_refs):
    for src_ref, dst_ref in zip(src_refs, dst_refs):
        if len(dst_ref.shape) == 3:
            width = dst_ref.shape[2]
            for c in range(dst_ref.shape[0]):
                dst_ref[c] = src_ref[:, c * width:(c + 1) * width].astype(BF16)
        else:
            dst_ref[...] = src_ref[...].astype(BF16)


def _slab_specs(weights, col_blocks, batch, heads):
    in_specs, out_specs, out_shapes = [], [], []
    for w, width in zip(weights, col_blocks):
        rows = w.shape[0] // (batch * heads)
        in_specs.append(pl.BlockSpec((rows, w.shape[1]), lambda bi, h: (bi * heads + h, 0)))
        if width is None:
            out_specs.append(pl.BlockSpec((rows, w.shape[1]), lambda bi, h: (bi * heads + h, 0)))
            out_shapes.append(jax.ShapeDtypeStruct(w.shape, BF16))
        else:
            n_blocks = w.shape[1] // width
            out_specs.append(pl.BlockSpec((n_blocks, rows, width), lambda bi, h: (0, bi * heads + h, 0)))
            out_shapes.append(jax.ShapeDtypeStruct((n_blocks, w.shape[0], width), BF16))
    return in_specs, out_specs, out_shapes


def _sb_pieces(q_ref, k_ref, v_ref, o_ref):
    t = ATTN_TILE
    row = lax.broadcasted_iota(jnp.int32, (t, t), 0)
    col = lax.broadcasted_iota(jnp.int32, (t, t), 1)
    mask = col < row
    tri = (row >= col).astype(BF16)
    tri2 = jnp.concatenate([tri, tri], axis=0)

    n_tiles = q_ref.shape[0] // t

    def logits(qi):
        return _dot_nt(q_ref[qi * t:(qi + 1) * t, :], k_ref[0:(qi + 1) * t, :])

    def suffix_sums(zj, diag):
        sp = _softplus2(zj)
        if diag:
            sp = jnp.where(mask, sp, 0.0)
        hi = sp.astype(BF16)
        lo = (sp - hi.astype(F32)).astype(BF16)
        return _dot(jnp.concatenate([hi, lo], axis=1), tri2)

    z = [logits(0)] + [None] * (n_tiles - 1)
    a_tiles = [[None] * (qi + 1) for qi in range(n_tiles)]
    carry = [None] * n_tiles

    def finish(qi, j, zj, local):
        cum = local if carry[qi] is None else local + carry[qi]
        a = jnp.exp2(zj - cum)
        if j == qi:
            a = jnp.where(mask, a, 0.0)
        a_tiles[qi][j] = a.astype(BF16)
        carry[qi] = local[:, 0:1] if carry[qi] is None else carry[qi] + local[:, 0:1]
        if j == 0:
            a_all = a_tiles[qi][0] if qi == 0 else jnp.concatenate(a_tiles[qi], axis=1)
            o_ref[qi * t:(qi + 1) * t, :] = _dot(a_all, v_ref[0:(qi + 1) * t, :])

    pending = None
    for qi in range(n_tiles):
        for j in range(qi, -1, -1):
            if j == qi and qi + 1 < n_tiles:
                z[qi + 1] = logits(qi + 1)
            zj = z[qi][:, j * t:(j + 1) * t]
            local = suffix_sums(zj, j == qi)
            if pending is not None:
                finish(*pending)
            pending = (qi, j, zj, local)
            yield
    finish(*pending)
    yield


def _mla_pieces(q_ref, kn_ref, kr_ref, v_ref, o_ref):
    t = ATTN_TILE
    row = lax.broadcasted_iota(jnp.int32, (t, t), 0)
    col = lax.broadcasted_iota(jnp.int32, (t, t), 1)
    mask = (col // CHUNK) <= (row // CHUNK)

    def scores(qi):
        n = qi + 1
        kcat = jnp.concatenate([kn_ref[0:n * t, :], kr_ref[0:n * t, :]], axis=1)
        return _dot_nt(q_ref[qi * t:n * t, :], kcat)

    n_tiles = q_ref.shape[0] // t
    sc_next = scores(0)
    for qi in range(n_tiles):
        n = qi + 1
        sc = sc_next
        if n < n_tiles:
            sc_next = scores(n)
        yield
        diag = jnp.where(mask, sc[:, qi * t:], -jnp.inf)
        sc = diag if n == 1 else jnp.concatenate([sc[:, :qi * t], diag], axis=1)
        p = jnp.exp2(sc - jnp.max(sc, axis=-1, keepdims=True))
        yield
        vcat = jnp.concatenate([v_ref[0:n * t, :], jnp.ones((n * t, LANES), BF16)], axis=1)
        acc = _dot(p.astype(BF16), vcat)
        o_ref[qi * t:n * t, :] = acc[:, :MLA_V_DIM] / acc[:, MLA_V_DIM:]
        yield


def _attn_kernel(n_cast, qs_ref, ks_ref, vs_ref, qm_ref, kn_ref, kr_ref, vm_ref, *refs):
    osb_ref, omla_ref = refs[n_cast], refs[n_cast + 1]
    _cast_slabs(refs[:n_cast], refs[n_cast + 2:])
    n_tiles = qs_ref.shape[0] // ATTN_TILE
    sb = _sb_pieces(qs_ref, ks_ref, vs_ref, osb_ref)
    mla = _mla_pieces(qm_ref, kn_ref, kr_ref, vm_ref, omla_ref)
    mla_per_tile = 3
    for qi in range(n_tiles):
        order = sorted([(k / (qi + 1), 0) for k in range(qi + 1)] +
                       [((k + 0.5) / mla_per_tile, 1) for k in range(mla_per_tile)])
        for _, stream in order:
            next(mla if stream else sb)
    for _ in sb:
        pass
    for _ in mla:
        pass


def _attention(qs, ks, vs, qm, kn, kr, vm, weights, col_blocks):
    b, s, _ = qs.shape
    head = lambda w: pl.BlockSpec((None, s, w), lambda bi, h: (bi, 0, h))
    slab_in, slab_out, slab_shapes = _slab_specs(weights, col_blocks, b, SB_HEADS)
    return pl.pallas_call(
        functools.partial(_attn_kernel, len(weights)),
        grid=(b, SB_HEADS),
        in_specs=[head(SB_HEAD_DIM)] * 3 +
                 [head(MLA_QK_PAD), head(MLA_NOPE_DIM), pl.BlockSpec((None, s, LANES), lambda bi, h: (bi, 0, 0)),
                  head(MLA_V_DIM)] + slab_in,
        out_specs=[head(SB_HEAD_DIM), head(MLA_V_DIM)] + slab_out,
        out_shape=[jax.ShapeDtypeStruct((b, s, SB_WIDTH), F32), jax.ShapeDtypeStruct((b, s, MLA_WIDTH), F32)] +
                  slab_shapes,
        compiler_params=pltpu.CompilerParams(dimension_semantics=("arbitrary",) * 2, vmem_limit_bytes=VMEM_LIMIT),
        name="attn",
    )(qs, ks, vs, qm, kn, kr, vm, *weights)


def _out_kernel(osb_ref, omla_ref, x_ref, g_sb_ref, g_mla_ref, w_o_ref, g_post_ref, g_ffn_ref, w_src_ref,
                x1_ref, h2_ref, w_dst_ref):
    w_dst_ref[...] = w_src_ref[...].astype(BF16)
    rows = osb_ref.shape[0]
    chunks = [slice(r, r + OUT_CHUNK) for r in range(0, rows, OUT_CHUNK)]

    def product(rs):
        n_sb = _rmsnorm(osb_ref[rs, :], g_sb_ref[...]).astype(BF16)
        n_mla = _rmsnorm(omla_ref[rs, :], g_mla_ref[...]).astype(BF16)
        return _dot(jnp.concatenate([n_sb, n_mla], axis=1), w_o_ref[...])

    def finish(rs, y):
        x1 = x_ref[rs, :] + _rmsnorm(y, g_post_ref[...])
        x1_ref[rs, :] = x1
        h2_ref[rs, :] = _rmsnorm(x1, g_ffn_ref[...]).astype(BF16)

    pending = None
    for rs in chunks:
        y = product(rs)
        if pending is not None:
            finish(*pending)
        pending = (rs, y)
    finish(*pending)


def _out_proj(o_sb, o_mla, x2, g_sb, g_mla, w_o, g_post, g_ffn, w_down):
    n = x2.shape[0]
    tm = OUT_ROWS
    steps = n // tm
    row = lambda w: pl.BlockSpec((tm, w), lambda i: (i, 0))
    full = _resident
    slab = pl.BlockSpec((w_down.shape[0] // steps, w_down.shape[1]), lambda i: (i, 0))
    return pl.pallas_call(
        _out_kernel,
        grid=(steps,),
        in_specs=[row(SB_WIDTH), row(MLA_WIDTH), row(D_MODEL), full(g_sb), full(g_mla), full(w_o), full(g_post),
                  full(g_ffn), slab],
        out_specs=[row(D_MODEL), row(D_MODEL), slab],
        out_shape=[jax.ShapeDtypeStruct((n, D_MODEL), F32), jax.ShapeDtypeStruct((n, D_MODEL), BF16),
                   jax.ShapeDtypeStruct(w_down.shape, BF16)],
        compiler_params=pltpu.CompilerParams(dimension_semantics=("arbitrary",), vmem_limit_bytes=VMEM_LIMIT),
        name="out_proj",
    )(o_sb, o_mla, x2, g_sb, g_mla, w_o, g_post, g_ffn, w_down)


def _ffn_kernel(seq_tiles, h_ref, x1_ref, wg_ref, wv_ref, cpg_ref, cpv_ref, wd_ref,
                g_post_ref, o_ref, pg_ref, pv_ref, tail_ref, act_ref, acc_ref):
    tm = FFN_ROWS
    i = pl.program_id(0)
    j = pl.program_id(1)
    last = pl.num_programs(1) - 1
    first = (i % seq_tiles) == 0

    def project(p_ref, which, w_ref):
        prev = tail_ref[j, which]
        p_ref[0:SUBLANES, :] = jnp.where(first, jnp.zeros_like(prev), prev)
        p_ref[SUBLANES:, :] = _dot(h_ref[...], w_ref[...])
        tail_ref[j, which] = p_ref[tm:tm + SUBLANES, :]

    def conv(p_ref, cp_ref):
        u = cp_ref[CONV_WIDTH:CONV_WIDTH + 1, :] + cp_ref[CONV_WIDTH - 1:CONV_WIDTH, :] * p_ref[SUBLANES:SUBLANES + tm, :]
        for tap in range(1, CONV_WIDTH):
            k = CONV_WIDTH - 1 - tap
            u = u + cp_ref[k:k + 1, :] * p_ref[SUBLANES - tap:SUBLANES - tap + tm, :]
        return u

    def build(slot, fill=(lambda: None, lambda: None)):
        project(pg_ref, 0, wg_ref)
        fill[0]()
        gate = jax.nn.gelu(conv(pg_ref, cpg_ref), approximate=True)
        project(pv_ref, 1, wv_ref)
        fill[1]()
        act_ref[slot] = (gate * conv(pv_ref, cpv_ref)).astype(BF16)

    def multiply(slot, cs=slice(None)):
        return _dot(act_ref[slot], wd_ref[:, cs])

    @pl.when(jnp.logical_and(i == 0, j == 0))
    def _():
        tail_ref[...] = jnp.zeros_like(tail_ref)

    @pl.when(j == 0)
    def _():
        acc_ref[...] = jnp.zeros_like(acc_ref)
        build(0)

    @pl.when(jnp.logical_and(j > 0, j < last))
    def _():
        slot = j % 2
        split = FFN_FILL_SPLIT
        shares = (slice(0, split), slice(split, D_MODEL))

        def accumulate(cs):
            def run():
                acc_ref[:, cs] += multiply(1 - slot, cs)
            return run

        build(slot, [accumulate(cs) for cs in shares])

    @pl.when(j == last)
    def _():
        y = acc_ref[...] + multiply((last - 1) % 2)
        o_ref[...] = x1_ref[...] + _rmsnorm(y, g_post_ref[...])


def _ffn(h2, x1, w_up_blocks, conv_w, conv_b, w_down, g_post, seq_len):
    n = h2.shape[0]
    conv_params = jnp.concatenate(
        [conv_w, conv_b, jnp.zeros((SUBLANES - CONV_WIDTH - 1, conv_w.shape[1]), conv_w.dtype)], axis=0)
    tm, tn = FFN_ROWS, FFN_COLS
    nj = D_FF // tn
    up_t = lambda j: jnp.minimum(j, nj - 1)
    return pl.pallas_call(
        functools.partial(_ffn_kernel, seq_len // tm),
        grid=(n // tm, nj + 1),
        in_specs=[pl.BlockSpec((tm, D_MODEL), lambda i, j: (i, 0)),
                  pl.BlockSpec((tm, D_MODEL), lambda i, j: (i, 0)),
                  pl.BlockSpec((None, D_MODEL, tn), lambda i, j: (up_t(j), 0, 0)),
                  pl.BlockSpec((None, D_MODEL, tn), lambda i, j: (up_t(j) + nj, 0, 0)),
                  pl.BlockSpec((SUBLANES, tn), lambda i, j: (0, up_t(j))),
                  pl.BlockSpec((SUBLANES, tn), lambda i, j: (0, up_t(j) + nj)),
                  pl.BlockSpec((tn, D_MODEL), lambda i, j: (jnp.maximum(j - 1, 0), 0)),
                  pl.BlockSpec((1, D_MODEL), lambda i, j: (0, 0))],
        out_specs=pl.BlockSpec((tm, D_MODEL), lambda i, j: (i, 0)),
        out_shape=jax.ShapeDtypeStruct((n, D_MODEL), F32),
        scratch_shapes=[pltpu.VMEM((tm + SUBLANES, tn), F32), pltpu.VMEM((tm + SUBLANES, tn), F32),
                        pltpu.VMEM((nj, 2, SUBLANES, tn), F32), pltpu.VMEM((2, tm, tn), BF16),
                        pltpu.VMEM((tm, D_MODEL), F32)],
        compiler_params=pltpu.CompilerParams(dimension_semantics=("arbitrary", "arbitrary"),
                                             vmem_limit_bytes=VMEM_LIMIT),
        name="ffn",
    )(h2, x1, w_up_blocks, w_up_blocks, conv_params, conv_params, w_down, g_post)


def _swap_halves(w):
    half = w.shape[-1] // 2
    return jnp.concatenate([w[..., half:], w[..., :half]], axis=-1)


def _layout_w_kr(w_in):
    k_rope = w_in[:, _C_KR:_C_KR + MLA_ROPE_DIM]
    return jnp.concatenate([k_rope, _swap_halves(k_rope)], axis=1).astype(BF16)


def _layout_w_uq(w_uq):
    w = w_uq.reshape(MLA_Q_RANK, MLA_HEADS, MLA_QK_DIM)
    rope = w[:, :, MLA_NOPE_DIM:]
    w = jnp.concatenate([w[:, :, :MLA_NOPE_DIM], rope, _swap_halves(rope)], axis=-1)
    return w.reshape(MLA_Q_RANK, MLA_HEADS * MLA_QK_PAD).astype(BF16)


def _layout_w_ukv(w_ukv):
    w = w_ukv.reshape(MLA_KV_RANK, MLA_HEADS, MLA_NOPE_DIM + MLA_V_DIM)
    k_nope = w[:, :, :MLA_NOPE_DIM].reshape(MLA_KV_RANK, MLA_WIDTH)
    v = w[:, :, MLA_NOPE_DIM:].reshape(MLA_KV_RANK, MLA_WIDTH)
    return jnp.concatenate([k_nope, v], axis=1).astype(BF16)


def kernel(x, positions, g_attn_pre, w_in, g_cq, w_uq, g_ckv, w_ukv, g_out_sb, g_out_mla, w_o, g_attn_post,
           g_ffn_pre, w_up, conv_w, conv_b, w_down, g_ffn_post):
    b, s, d = x.shape
    depth = w_in.shape[0]
    n = b * s
    half = MLA_ROPE_DIM // 2
    inv_freq = ROPE_THETA ** (-jnp.arange(half, dtype=F32) / half)
    invf = jnp.tile(inv_freq, LANES // half)[None, :]
    pos2 = positions.reshape(n, 1)
    x2 = x.reshape(n, d)
    r2 = lambda g: g[None, :]
    for l in range(depth):
        qsb, ksb, vsb, qm, kn, kr, vm = _proj(
            x2, pos2, invf, r2(g_attn_pre[l]), w_in[l].astype(BF16), _layout_w_kr(w_in[l]), r2(g_cq[l]),
            _layout_w_uq(w_uq[l]), r2(g_ckv[l]), _layout_w_ukv(w_ukv[l]))
        b3 = lambda a: a.reshape(b, s, a.shape[-1])
        o_sb, o_mla, w_up_b, w_o_b = _attention(
            b3(qsb), b3(ksb), b3(vsb), b3(qm), b3(kn), b3(kr), b3(vm), [w_up[l], w_o[l]], [FFN_COLS, None])
        x1, h2, w_down_b = _out_proj(o_sb.reshape(n, SB_WIDTH), o_mla.reshape(n, MLA_WIDTH), x2, r2(g_out_sb[l]),
                                     r2(g_out_mla[l]), w_o_b, r2(g_attn_post[l]), r2(g_ffn_pre[l]), w_down[l])
        x2 = _ffn(h2, x1, w_up_b, conv_w[l], r2(conv_b[l]), w_down_b, r2(g_ffn_post[l]), s)
    return x2.reshape(b, s, d)
```
